```python
import math
import jax
import jax.numpy as jnp
from jax import lax
import numpy as np

D_MODEL = 1024
BATCH = 4
SEQ = 8192
DEPTH = 2

HEAD_DIM = 64
Q_BLOCK = 128
N_FOX_HEADS = 8
DIL_PAIRS = ((128, 1), (512, 4), (2048, 16))
N_DIL_GROUPS = 3
N_DIL_SLOTS = 4
N_DIL_HEADS = N_DIL_GROUPS * N_DIL_SLOTS
N_DSA_HEADS = 16
KV_RANK = 256
N_IDX_HEADS = 8
IDX_DIM = 64
DSA_TOPK = 256
N_BUCKETS = 32
MAX_DISTANCE = 2048
N_BIAS_HEADS = 16
N_EXPERTS = 32
TOP_K = 4
D_EXPERT = 1024
SWIGLU_ALPHA = 1.702
SWIGLU_LIMIT = 7.0
MOE_CHUNK = 256
EPS = 1e-6

FOX_QKV = 3 * N_FOX_HEADS * HEAD_DIM
DIL_QKV = 3 * N_DIL_HEADS * HEAD_DIM
EVEN_IN = FOX_QKV + N_FOX_HEADS + DIL_QKV
EVEN_OUT = (N_FOX_HEADS + N_DIL_SLOTS) * HEAD_DIM
DSA_Q = N_DSA_HEADS * HEAD_DIM
IDX_Q = N_IDX_HEADS * IDX_DIM
ODD_IN = DSA_Q + KV_RANK + IDX_Q + IDX_DIM + N_IDX_HEADS
ODD_OUT = DSA_Q

kernel_name = "hybrid_fox_dilated_dsa_moe_adaln"


def rmsnorm(x, g):
    xf = x.astype(jnp.float32)
    y = xf * lax.rsqrt(jnp.mean(xf * xf, axis=-1, keepdims=True) + EPS)
    return (y * g.astype(jnp.float32)).astype(x.dtype)


def t5_bucket(dist):
    max_exact = N_BUCKETS // 2
    d = jnp.maximum(dist, 0)
    df = jnp.maximum(d, 1).astype(jnp.float32)
    large = max_exact + (jnp.log(df / max_exact) / math.log(MAX_DISTANCE / max_exact)
                         * (N_BUCKETS - max_exact)).astype(jnp.int32)
    large = jnp.minimum(large, N_BUCKETS - 1)
    return jnp.where(d < max_exact, d, large)


def forgetting_attention(q, k, v, log_f):
    b, s, h, hd = q.shape
    scale = hd ** -0.5
    cum = jnp.cumsum(log_f, axis=1).transpose(0, 2, 1)
    kpos = jnp.arange(s)

    def block(i):
        start = i * Q_BLOCK
        qpos = start + jnp.arange(Q_BLOCK)
        qb = lax.dynamic_slice_in_dim(q, start, Q_BLOCK, axis=1)
        cq = lax.dynamic_slice_in_dim(cum, start, Q_BLOCK, axis=2)
        sc = jnp.einsum('bqhd,bkhd->bhqk', qb, k, preferred_element_type=jnp.float32) * scale
        sc = sc + (cq[..., :, None] - cum[..., None, :])
        sc = jnp.where(kpos[None, :] <= qpos[:, None], sc, -jnp.inf)
        p = jax.nn.softmax(sc, axis=-1)
        return jnp.einsum('bhqk,bkhd->bqhd', p.astype(v.dtype), v)

    out = lax.map(block, jnp.arange(s // Q_BLOCK))
    return out.transpose(1, 0, 2, 3, 4).reshape(b, s, h, hd)


def dilated_attention(q, k, v, rel_bias):
    b, s, _, _, hd = q.shape
    scale = hd ** -0.5
    ks = [k[:, :, g] for g in range(N_DIL_GROUPS)]
    vs = [v[:, :, g] for g in range(N_DIL_GROUPS)]
    offsets, biases = [], []
    for g, (window, dil) in enumerate(DIL_PAIRS):
        dist = jnp.arange(window // dil + 1) * dil
        offsets.append(dist)
        tab = rel_bias[t5_bucket(dist)][:, g * N_DIL_SLOTS:(g + 1) * N_DIL_SLOTS]
        biases.append(tab.T.astype(jnp.float32))

    def block(i):
        start = i * Q_BLOCK
        qpos = start + jnp.arange(Q_BLOCK)
        qb = lax.dynamic_slice_in_dim(q, start, Q_BLOCK, axis=1)
        outs, lses = [], []
        for g in range(N_DIL_GROUPS):
            kp = qpos[:, None] - offsets[g][None, :]
            valid = kp >= 0
            kidx = jnp.maximum(kp, 0)
            kg = ks[g][:, kidx]
            vg = vs[g][:, kidx]
            sc = jnp.einsum('bqjd,bqnjd->bjqn', qb[:, :, g], kg,
                            preferred_element_type=jnp.float32) * scale
            sc = jnp.where(valid[None, None], sc + biases[g][None, :, None, :], -jnp.inf)
            mx = jnp.max(sc, axis=-1, keepdims=True)
            e = jnp.exp(sc - mx)
            den = jnp.sum(e, axis=-1, keepdims=True)
            outs.append(jnp.einsum('bjqn,bqnjd->bqjd', (e / den).astype(vg.dtype), vg))
            lses.append((mx + jnp.log(den))[..., 0])
        alpha = jax.nn.softmax(jnp.stack(lses), axis=0)
        return sum(alpha[g].transpose(0, 2, 1)[..., None].astype(outs[g].dtype) * outs[g]
                   for g in range(N_DIL_GROUPS))

    out = lax.map(block, jnp.arange(s // Q_BLOCK))
    return out.transpose(1, 0, 2, 3, 4).reshape(b, s, N_DIL_SLOTS, hd)


def dsa_attention(q, k, v, q_idx, k_idx, w_idx, rel_bias):
    b, s, h, hd = q.shape
    n_sel = min(DSA_TOPK, s // 4)
    scale = hd ** -0.5
    idx_scale = (N_IDX_HEADS * IDX_DIM) ** -0.5
    kpos = jnp.arange(s)

    def block(i):
        start = i * Q_BLOCK
        qpos = start + jnp.arange(Q_BLOCK)
        qi = lax.dynamic_slice_in_dim(q_idx, start, Q_BLOCK, axis=1)
        wi = lax.dynamic_slice_in_dim(w_idx, start, Q_BLOCK, axis=1)
        dots = jnp.einsum('bqhd,bkd->bqhk', qi, k_idx, preferred_element_type=jnp.float32)
        score = jnp.einsum('bqh,bqhk->bqk', wi.astype(jnp.float32), jax.nn.relu(dots)) * idx_scale
        score = jnp.where(kpos[None, None, :] <= qpos[None, :, None], score, -jnp.inf)
        _, sel = lax.top_k(score, n_sel)
        valid = sel <= qpos[None, :, None]
        kb = jax.vmap(lambda kk, ss: kk[ss])(k, sel)
        vb = jax.vmap(lambda vv, ss: vv[ss])(v, sel)
        qb = lax.dynamic_slice_in_dim(q, start, Q_BLOCK, axis=1)
        sc = jnp.einsum('bqhd,bqnhd->bhqn', qb, kb, preferred_element_type=jnp.float32) * scale
        bias = rel_bias[t5_bucket(qpos[None, :, None] - sel)].astype(jnp.float32)
        sc = jnp.where(valid[:, None], sc + bias.transpose(0, 3, 1, 2), -jnp.inf)
        p = jax.nn.softmax(sc, axis=-1)
        return jnp.einsum('bhqn,bqnhd->bqhd', p.astype(vb.dtype), vb)

    out = lax.map(block, jnp.arange(s // Q_BLOCK))
    return out.transpose(1, 0, 2, 3, 4).reshape(b, s, h, hd)


def even_mixer(h, w_in, fox_fb, w_out, rel_bias):
    b, s, _ = h.shape
    proj = h @ w_in
    fox_qkv, fox_f, dil_qkv = jnp.split(proj, [FOX_QKV, FOX_QKV + N_FOX_HEADS], axis=-1)
    fq, fk, fv = (t.reshape(b, s, N_FOX_HEADS, HEAD_DIM) for t in jnp.split(fox_qkv, 3, axis=-1))
    log_f = jax.nn.log_sigmoid((fox_f + fox_fb).astype(jnp.float32))
    fox_out = forgetting_attention(fq, fk, fv, log_f)
    dq, dk, dv = (t.reshape(b, s, N_DIL_GROUPS, N_DIL_SLOTS, HEAD_DIM)
                  for t in jnp.split(dil_qkv, 3, axis=-1))
    dil_out = dilated_attention(dq, dk, dv, rel_bias)
    merged = jnp.concatenate([fox_out.reshape(b, s, -1), dil_out.reshape(b, s, -1)], axis=-1)
    return merged @ w_out


def odd_mixer(h, w_in, kv_norm, w_ukv, w_out, rel_bias):
    b, s, _ = h.shape
    proj = h @ w_in
    q, ckv, qi, ki, wi = jnp.split(
        proj, [DSA_Q, DSA_Q + KV_RANK, DSA_Q + KV_RANK + IDX_Q, DSA_Q + KV_RANK + IDX_Q + IDX_DIM],
        axis=-1)
    kv = rmsnorm(ckv, kv_norm) @ w_ukv
    k, v = (t.reshape(b, s, N_DSA_HEADS, HEAD_DIM) for t in jnp.split(kv, 2, axis=-1))
    q = q.reshape(b, s, N_DSA_HEADS, HEAD_DIM)
    qi = qi.reshape(b, s, N_IDX_HEADS, IDX_DIM)
    out = dsa_attention(q, k, v, qi, ki, wi, rel_bias)
    return out.reshape(b, s, -1) @ w_out


def moe_ffn(h, router_w, router_b, w1, b1, w2, b2):
    b, s, d = h.shape
    xt = h.reshape(-1, d)
    n = xt.shape[0]
    logits = (xt @ router_w + router_b).astype(jnp.float32)
    top_vals, top_idx = lax.top_k(logits, TOP_K)
    gates = jax.nn.softmax(top_vals, axis=-1)
    flat_e = top_idx.reshape(-1)
    flat_tok = jnp.repeat(jnp.arange(n, dtype=jnp.int32), TOP_K)
    flat_g = gates.reshape(-1)
    order = jnp.argsort(flat_e)
    e_sorted, tok_sorted, g_sorted = flat_e[order], flat_tok[order], flat_g[order]
    counts = jnp.bincount(flat_e, length=N_EXPERTS)
    padded = ((counts + MOE_CHUNK - 1) // MOE_CHUNK) * MOE_CHUNK
    start_raw = jnp.cumsum(counts) - counts
    ends_pad = jnp.cumsum(padded)
    start_pad = ends_pad - padded
    dest = start_pad[e_sorted] + (jnp.arange(n * TOP_K) - start_raw[e_sorted])
    n_chunks = -(-(n * TOP_K) // MOE_CHUNK) + N_EXPERTS
    rows = n_chunks * MOE_CHUNK
    row_tok = jnp.full((rows,), n, dtype=jnp.int32).at[dest].set(tok_sorted)
    row_gate = jnp.zeros((rows,), jnp.float32).at[dest].set(g_sorted)
    chunk_exp = jnp.minimum(
        jnp.searchsorted(ends_pad, jnp.arange(n_chunks) * MOE_CHUNK, side='right'), N_EXPERTS - 1)
    x_pad = jnp.concatenate([xt, jnp.zeros((1, d), xt.dtype)], axis=0)

    def run_chunk(args):
        toks, e = args
        hm = x_pad[toks] @ w1[e] + b1[e]
        glu = jnp.minimum(hm[:, :D_EXPERT], SWIGLU_LIMIT)
        lin = jnp.clip(hm[:, D_EXPERT:], -SWIGLU_LIMIT, SWIGLU_LIMIT)
        act = glu * jax.nn.sigmoid(SWIGLU_ALPHA * glu) * (lin + 1.0)
        return act @ w2[e] + b2[e]

    y = lax.map(run_chunk, (row_tok.reshape(n_chunks, MOE_CHUNK), chunk_exp))
    y = y.reshape(rows, d) * row_gate[:, None].astype(y.dtype)
    out = jax.ops.segment_sum(y, row_tok, num_segments=n + 1)[:n]
    return out.reshape(b, s, d)


def setup_inputs(seed: int = 0) -> dict:
    key = jax.random.key(seed)
    ks = iter(jax.random.split(key, 40))
    nrm = lambda shape, sc: jax.random.normal(next(ks), shape, jnp.float32) * sc
    d = D_MODEL
    return {
        "x": nrm((BATCH, SEQ, d), 1.0),
        "c": nrm((BATCH, d), 1.0),
        "rel_bias": nrm((N_BUCKETS, N_BIAS_HEADS), 0.5),
        "l0_norm1": 1.0 + nrm((d,), 0.02),
        "l0_ada_w": nrm((d, 6 * d), 0.5 * d ** -0.5),
        "l0_ada_b": nrm((6 * d,), 0.02),
        "l0_w_in": nrm((d, EVEN_IN), d ** -0.5),
        "l0_fox_fb": 4.0 + nrm((N_FOX_HEADS,), 0.5),
        "l0_w_out": nrm((EVEN_OUT, d), EVEN_OUT ** -0.5),
        "l0_norm2": 1.0 + nrm((d,), 0.02),
        "l0_router_w": nrm((d, N_EXPERTS), d ** -0.5),
        "l0_router_b": nrm((N_EXPERTS,), 0.01),
        "l0_w1": nrm((N_EXPERTS, d, 2 * D_EXPERT), d ** -0.5),
        "l0_b1": nrm((N_EXPERTS, 2 * D_EXPERT), 0.01),
        "l0_w2": nrm((N_EXPERTS, D_EXPERT, d), D_EXPERT ** -0.5),
        "l0_b2": nrm((N_EXPERTS, d), 0.01),
        "l1_norm1": 1.0 + nrm((d,), 0.02),
        "l1_ada_w": nrm((d, 6 * d), 0.5 * d ** -0.5),
        "l1_ada_b": nrm((6 * d,), 0.02),
        "l1_w_in": nrm((d, ODD_IN), d ** -0.5),
        "l1_kv_norm": 1.0 + nrm((KV_RANK,), 0.02),
        "l1_w_ukv": nrm((KV_RANK, 2 * N_DSA_HEADS * HEAD_DIM), KV_RANK ** -0.5),
        "l1_w_out": nrm((ODD_OUT, d), ODD_OUT ** -0.5),
        "l1_norm2": 1.0 + nrm((d,), 0.02),
        "l1_router_w": nrm((d, N_EXPERTS), d ** -0.5),
        "l1_router_b": nrm((N_EXPERTS,), 0.01),
        "l1_w1": nrm((N_EXPERTS, d, 2 * D_EXPERT), d ** -0.5),
        "l1_b1": nrm((N_EXPERTS, 2 * D_EXPERT), 0.01),
        "l1_w2": nrm((N_EXPERTS, D_EXPERT, d), D_EXPERT ** -0.5),
        "l1_b2": nrm((N_EXPERTS, d), 0.01),
        "final_norm": 1.0 + nrm((d,), 0.02),
    }


def reference(x, c, rel_bias,
              l0_norm1, l0_ada_w, l0_ada_b, l0_w_in, l0_fox_fb, l0_w_out,
              l0_norm2, l0_router_w, l0_router_b, l0_w1, l0_b1, l0_w2, l0_b2,
              l1_norm1, l1_ada_w, l1_ada_b, l1_w_in, l1_kv_norm, l1_w_ukv, l1_w_out,
              l1_norm2, l1_router_w, l1_router_b, l1_w1, l1_b1, l1_w2, l1_b2,
              final_norm):
    even_params = (l0_w_in, l0_fox_fb, l0_w_out)
    odd_params = (l1_w_in, l1_kv_norm, l1_w_ukv, l1_w_out)
    common = (
        (l0_norm1, l0_ada_w, l0_ada_b, l0_norm2, l0_router_w, l0_router_b, l0_w1, l0_b1, l0_w2, l0_b2),
        (l1_norm1, l1_ada_w, l1_ada_b, l1_norm2, l1_router_w, l1_router_b, l1_w1, l1_b1, l1_w2, l1_b2),
    )
    for i in range(DEPTH):
        norm1, ada_w, ada_b, norm2, rw, rb, w1, b1, w2, b2 = common[i]
        mods = (jax.nn.silu(c) @ ada_w + ada_b)[:, None, :]
        shift1, scale1, gate1, shift2, scale2, gate2 = jnp.split(mods, 6, axis=-1)
        h = rmsnorm(x, norm1) * (1.0 + scale1) + shift1
        if i % 2 == 0:
            mix = even_mixer(h, *even_params, rel_bias)
        else:
            mix = odd_mixer(h, *odd_params, rel_bias)
        x = x + gate1 * mix
        h = rmsnorm(x, norm2) * (1.0 + scale2) + shift2
        x = x + gate2 * moe_ffn(h, rw, rb, w1, b1, w2, b2)
    return rmsnorm(x, final_norm)
```

```python
import functools
import math

import numpy as np
import jax
import jax.numpy as jnp
from jax import lax
from jax.experimental import pallas as pl
from jax.experimental.pallas import tpu as pltpu

F32, BF16, I32 = jnp.float32, jnp.bfloat16, jnp.int32

LANES = 128
VMEM_LIMIT_BYTES = 56 * 1024 * 1024

HEAD_DIM = 64
N_FOX_HEADS = 8
DIL_PAIRS = ((128, 1), (512, 4), (2048, 16))
N_DIL_SLOTS = 4
N_DSA_HEADS = 16
KV_RANK = 256
N_IDX_HEADS = 8
DSA_TOPK = 256
N_BUCKETS = 32
MAX_DISTANCE = 2048
N_EXPERTS = 32
TOP_K = 4
D_EXPERT = 1024
SWIGLU_ALPHA = 1.702
SWIGLU_LIMIT = 7.0
MOE_CHUNK = 256
EPS = 1e-6

NEG = -1e30
INT_MIN = -(2 ** 31)
QK_SCALE = HEAD_DIM ** -0.5
FOX_PARTS = 3
FOX_XW = FOX_PARTS * N_FOX_HEADS


def _cparams(sem, vmem=None):
    return pltpu.CompilerParams(dimension_semantics=sem, vmem_limit_bytes=vmem)


def _dot_nt(a, b):
    return lax.dot_general(a, b, (((1,), (1,)), ((), ())), preferred_element_type=F32)


def _dot(a, b):
    return jnp.dot(a, b, preferred_element_type=F32)


def _dot_f32(a, b):
    return jnp.dot(a, b, precision=lax.Precision.HIGHEST, preferred_element_type=F32)


def _keep_lanes(x, sel):
    return jnp.where(sel, x.astype(F32), 0.0).astype(x.dtype)


def _lane_tile(x, n):
    return x if n == 1 else jnp.concatenate([x] * n, axis=1)


def _tri_schedule(nq, kq_ratio=1):
    qs, ks = [], []
    for qi in range(nq):
        for ki in range(qi // kq_ratio + 1):
            qs.append(qi)
            ks.append(ki)
    return jnp.asarray(qs, I32), jnp.asarray(ks, I32)


def _ada_kernel(c_ref, w_ref, b_ref, o_ref):
    c = c_ref[...]
    sc = c * jax.nn.sigmoid(c)
    o_ref[...] = _dot_f32(sc, w_ref[...]) + b_ref[...]


def ada_mods(c, w, b):
    bsz, d = c.shape
    n = w.shape[1]
    tn = min(n, 1024)
    cp = jnp.zeros((8, d), F32).at[:bsz].set(c)
    out = pl.pallas_call(
        _ada_kernel,
        out_shape=jax.ShapeDtypeStruct((8, n), F32),
        grid=(n // tn,),
        in_specs=[pl.BlockSpec((8, d), lambda j: (0, 0)),
                  pl.BlockSpec((d, tn), lambda j: (0, j)),
                  pl.BlockSpec((1, tn), lambda j: (0, j))],
        out_specs=pl.BlockSpec((8, tn), lambda j: (0, j)),
        compiler_params=_cparams(("arbitrary",)),
        name="ada_mods",
    )(cp, w, b.reshape(1, n))
    return out[:bsz]


def _nmm_kernel(x_ref, g_ref, sc_ref, sh_ref, w_ref, *rest, tn, has_aux):
    if has_aux:
        wa_ref, o_ref, oa_ref = rest
    else:
        (o_ref,) = rest
    x = x_ref[...]
    ms = jnp.mean(x * x, axis=-1, keepdims=True)
    y = x * lax.rsqrt(ms + EPS) * g_ref[...]
    hf = y * (1.0 + sc_ref[0]) + sh_ref[0]
    h = hf.astype(BF16)
    n = o_ref.shape[1]
    for j0 in range(0, n, tn):
        j1 = min(j0 + tn, n)
        o_ref[:, j0:j1] = _dot(h, w_ref[:, j0:j1]).astype(o_ref.dtype)
    if has_aux:
        na = oa_ref.shape[1]
        for j0 in range(0, na, tn):
            j1 = min(j0 + tn, na)
            oa_ref[:, j0:j1] = _dot_f32(hf, wa_ref[:, j0:j1])


def norm_mod_matmul(x2, g, scale, shift, w, w_aux, seq, tm=512, tn=256):
    n_tok, din = x2.shape
    nout = w.shape[1]
    tm = min(tm, seq)
    has_aux = w_aux is not None
    bsel = lambda i: (i * tm // seq, 0, 0)
    in_specs = [pl.BlockSpec((tm, din), lambda i: (i, 0)),
                pl.BlockSpec((1, din), lambda i: (0, 0)),
                pl.BlockSpec((1, 1, din), bsel),
                pl.BlockSpec((1, 1, din), bsel),
                pl.BlockSpec((din, nout), lambda i: (0, 0))]
    out_shape = [jax.ShapeDtypeStruct((n_tok, nout), BF16)]
    out_specs = [pl.BlockSpec((tm, nout), lambda i: (i, 0))]
    args = [x2, g.reshape(1, din), scale.reshape(-1, 1, din), shift.reshape(-1, 1, din), w]
    if has_aux:
        na = w_aux.shape[1]
        in_specs.append(pl.BlockSpec((din, na), lambda i: (0, 0)))
        out_shape.append(jax.ShapeDtypeStruct((n_tok, na), F32))
        out_specs.append(pl.BlockSpec((tm, na), lambda i: (i, 0)))
        args.append(w_aux)
    outs = pl.pallas_call(
        functools.partial(_nmm_kernel, tn=min(tn, nout), has_aux=has_aux),
        out_shape=out_shape,
        grid=(n_tok // tm,),
        in_specs=in_specs,
        out_specs=out_specs,
        compiler_params=_cparams(("arbitrary",), VMEM_LIMIT_BYTES),
        name="norm_mod_matmul",
    )(*args)
    return outs if has_aux else outs[0]


def _foxcum_kernel(z_ref, fb_ref, qx_ref, kx_ref, carry):
    t = pl.program_id(1)

    @pl.when(t == 0)
    def _():
        carry[...] = jnp.zeros_like(carry)

    ts = z_ref.shape[1]
    z = z_ref[0] + fb_ref[...]
    lf = jnp.minimum(z, 0.0) - jnp.log1p(jnp.exp(-jnp.abs(z)))
    row = lax.broadcasted_iota(I32, (ts, ts), 0)
    col = lax.broadcasted_iota(I32, (ts, ts), 1)
    tri = jnp.where(row >= col, 1.0, 0.0).astype(F32)
    c = jnp.dot(tri, lf, precision=lax.Precision.HIGHEST, preferred_element_type=F32) + carry[...]
    carry[...] = c[ts - 1:ts, :]
    hi = c.astype(BF16).astype(F32)
    r1 = c - hi
    mid = r1.astype(BF16).astype(F32)
    lo = (r1 - mid).astype(BF16).astype(F32)
    lane = lax.broadcasted_iota(I32, (ts, LANES), 1)
    ph = lane % FOX_PARTS
    part = jnp.where(ph == 0, hi, jnp.where(ph == 1, mid, lo))
    first = lane < FOX_XW
    second = (lane >= FOX_XW) & (lane < 2 * FOX_XW)
    qx_ref[0] = jnp.where(first, part, jnp.where(second, 1.0, 0.0)).astype(BF16)
    kx_ref[0] = jnp.where(first, 1.0, jnp.where(second, -part, 0.0)).astype(BF16)


def fox_gate_columns(z, fb, ts=512):
    bsz, seq, _ = z.shape
    ts = min(ts, seq)
    return pl.pallas_call(
        _foxcum_kernel,
        out_shape=[jax.ShapeDtypeStruct((bsz, seq, LANES), BF16)] * 2,
        grid=(bsz, seq // ts),
        in_specs=[pl.BlockSpec((1, ts, LANES), lambda b, t: (b, t, 0)),
                  pl.BlockSpec((1, LANES), lambda b, t: (0, 0))],
        out_specs=[pl.BlockSpec((1, ts, LANES), lambda b, t: (b, t, 0))] * 2,
        scratch_shapes=[pltpu.VMEM((1, LANES), F32)],
        compiler_params=_cparams(("arbitrary", "arbitrary")),
        name="fox_gate_columns",
    )(z, fb)


def _fox_kernel(qi_ref, ki_ref, q_ref, k_ref, v_ref, qx_ref, kx_ref, o_ref, m_scr, l_scr, acc_scr, *, T):
    p = pl.program_id(1)
    s = pl.program_id(2)
    qi = qi_ref[s]
    ki = ki_ref[s]

    @pl.when(ki == 0)
    def _():
        m_scr[...] = jnp.full_like(m_scr, NEG)
        l_scr[...] = jnp.zeros_like(l_scr)
        acc_scr[...] = jnp.zeros_like(acc_scr)

    lane = lax.broadcasted_iota(I32, (1, LANES), 1)
    lo_half = lane < HEAD_DIM

    def step(masked):
        q = q_ref[0] * QK_SCALE
        qx = qx_ref[0]
        kcat = jnp.concatenate([k_ref[0], kx_ref[0]], axis=1)
        v = v_ref[0]
        if masked:
            row = lax.broadcasted_iota(I32, (T, T), 0)
            col = lax.broadcasted_iota(I32, (T, T), 1)
            causal = col <= row
        pvs, alphas = [], []
        for hh in range(2):
            xl = FOX_PARTS * (2 * p + hh)
            qsel = lo_half if hh == 0 else jnp.logical_not(lo_half)
            xsel = ((lane >= xl) & (lane < xl + FOX_PARTS)) | (
                (lane >= FOX_XW + xl) & (lane < FOX_XW + xl + FOX_PARTS))
            qcat = jnp.concatenate([_keep_lanes(q, qsel), _keep_lanes(qx, xsel)], axis=1)
            sc = _dot_nt(qcat, kcat)
            if masked:
                sc = jnp.where(causal, sc, NEG)
            m_prev = m_scr[hh]
            m_new = jnp.maximum(m_prev, jnp.max(sc, axis=1, keepdims=True))
            alpha = jnp.exp(m_prev - m_new)
            pm = jnp.exp(sc - _lane_tile(m_new, T // LANES))
            l_scr[hh] = alpha * l_scr[hh] + jnp.sum(pm, axis=1, keepdims=True)
            m_scr[hh] = m_new
            pvs.append(_dot(pm.astype(BF16), v))
            alphas.append(alpha)
        acc_scr[...] = (acc_scr[...] * jnp.where(lo_half, alphas[0], alphas[1])
                        + jnp.where(lo_half, pvs[0], pvs[1]))

    @pl.when(ki < qi)
    def _():
        step(False)

    @pl.when(ki == qi)
    def _():
        step(True)
        l = jnp.where(lo_half, l_scr[0], l_scr[1])
        o_ref[0] = (acc_scr[...] / l).astype(o_ref.dtype)


def fox_attention(proj, qx, kx, n_heads, T=512):
    bsz, seq, _ = proj.shape
    T = min(T, seq)
    npair = n_heads // 2
    qs, ks = _tri_schedule(seq // T)
    grid_spec = pltpu.PrefetchScalarGridSpec(
        num_scalar_prefetch=2,
        grid=(bsz, npair, qs.shape[0]),
        in_specs=[
            pl.BlockSpec((1, T, LANES), lambda b, p, s, qi, ki: (b, qi[s], p)),
            pl.BlockSpec((1, T, LANES), lambda b, p, s, qi, ki: (b, ki[s], npair + p)),
            pl.BlockSpec((1, T, LANES), lambda b, p, s, qi, ki: (b, ki[s], 2 * npair + p)),
            pl.BlockSpec((1, T, LANES), lambda b, p, s, qi, ki: (b, qi[s], 0)),
            pl.BlockSpec((1, T, LANES), lambda b, p, s, qi, ki: (b, ki[s], 0)),
        ],
        out_specs=pl.BlockSpec((1, T, LANES), lambda b, p, s, qi, ki: (b, qi[s], p)),
        scratch_shapes=[pltpu.VMEM((2, T, LANES), F32), pltpu.VMEM((2, T, LANES), F32),
                        pltpu.VMEM((T, LANES), F32)],
    )
    return pl.pallas_call(
        functools.partial(_fox_kernel, T=T),
        out_shape=jax.ShapeDtypeStruct((bsz, seq, npair * LANES), BF16),
        grid_spec=grid_spec,
        compiler_params=_cparams(("arbitrary", "arbitrary", "arbitrary"), VMEM_LIMIT_BYTES),
        name="fox_attention",
    )(qs, ks, proj, proj, proj, qx, kx)


def _dil_kernel(q_ref, kp_ref, kc_ref, vp_ref, vc_ref, bias_ref, o_ref, lse_ref, *, W):
    ut = pl.program_id(3)
    lane = lax.broadcasted_iota(I32, (1, LANES), 1)
    lo_half = lane < HEAD_DIM
    q = q_ref[0] * QK_SCALE
    k = jnp.concatenate([kp_ref[0], kc_ref[0]], axis=0)
    v = jnp.concatenate([vp_ref[0], vc_ref[0]], axis=0)
    col = lax.broadcasted_iota(I32, (W, 2 * W), 1)
    no_prev = jnp.logical_and(ut == 0, col < W)
    outs, lses = [], []
    for hh in range(2):
        qsel = lo_half if hh == 0 else jnp.logical_not(lo_half)
        sc = _dot_nt(_keep_lanes(q, qsel), k) + bias_ref[0, hh]
        sc = jnp.where(no_prev, NEG, sc)
        mx = jnp.max(sc, axis=1, keepdims=True)
        e = jnp.exp(sc - mx)
        den = jnp.sum(e, axis=1, keepdims=True)
        outs.append(_dot((e / den).astype(BF16), v))
        lses.append(mx + jnp.log(den))
    o_ref[0] = jnp.where(lo_half, outs[0], outs[1])
    lse_ref[0] = jnp.where(lo_half, lses[0], lses[1])


def dilated_group_attention(proj, bias, g, dil, col0, W=128):
    bsz, seq, c = proj.shape
    sub = seq // dil
    pv = proj.reshape(bsz, sub, dil * c)
    cb = c // LANES
    qb = col0 // LANES
    kb = qb + 3 * N_DIL_SLOTS * HEAD_DIM // LANES
    vb = kb + 3 * N_DIL_SLOTS * HEAD_DIM // LANES
    prev = lambda u: jnp.maximum(u - 1, 0)
    out = pl.pallas_call(
        functools.partial(_dil_kernel, W=W),
        out_shape=[jax.ShapeDtypeStruct((bsz, sub, dil * 2 * LANES), F32)] * 2,
        grid=(bsz, dil, 2, sub // W),
        in_specs=[
            pl.BlockSpec((1, W, LANES), lambda b, a, p, u: (b, u, a * cb + qb + p)),
            pl.BlockSpec((1, W, LANES), lambda b, a, p, u: (b, prev(u), a * cb + kb + p)),
            pl.BlockSpec((1, W, LANES), lambda b, a, p, u: (b, u, a * cb + kb + p)),
            pl.BlockSpec((1, W, LANES), lambda b, a, p, u: (b, prev(u), a * cb + vb + p)),
            pl.BlockSpec((1, W, LANES), lambda b, a, p, u: (b, u, a * cb + vb + p)),
            pl.BlockSpec((1, 2, W, 2 * W), lambda b, a, p, u: (p, 0, 0, 0)),
        ],
        out_specs=[pl.BlockSpec((1, W, LANES), lambda b, a, p, u: (b, u, a * 2 + p))] * 2,
        compiler_params=_cparams(("arbitrary",) * 4),
        name=f"dilated_attention_g{g}",
    )(pv, pv, pv, pv, pv, bias)
    return [o.reshape(bsz * seq, 2 * LANES) for o in out]


def _t5_bucket(dist):
    max_exact = N_BUCKETS // 2
    d = jnp.maximum(dist, 0)
    df = jnp.maximum(d, 1).astype(F32)
    large = max_exact + (jnp.log(df / max_exact) / math.log(MAX_DISTANCE / max_exact)
                         * (N_BUCKETS - max_exact)).astype(I32)
    large = jnp.minimum(large, N_BUCKETS - 1)
    return jnp.where(d < max_exact, d, large)


def _dil_bias_tiles(rel_bias, g, window, dil, W=128):
    n_keys = window // dil + 1
    assert n_keys == W + 1
    tab = rel_bias[_t5_bucket(jnp.arange(n_keys) * dil)][:, g * N_DIL_SLOTS:(g + 1) * N_DIL_SLOTS]
    i = np.arange(W)[:, None]
    j = np.arange(2 * W)[None, :]
    n = i + W - j
    valid = (n >= 0) & (n <= W)
    tiles = jnp.where(valid[None], tab.T.astype(F32)[:, np.clip(n, 0, W)], NEG)
    return tiles.reshape(2, 2, W, 2 * W)


def _oproj0_kernel(fox_ref, o0_ref, o1_ref, o2_ref, l0_ref, l1_ref, l2_ref, w_ref, x_ref, gate_ref, out_ref):
    ls = [l0_ref[...], l1_ref[...], l2_ref[...]]
    os_ = [o0_ref[...], o1_ref[...], o2_ref[...]]
    mx = jnp.maximum(jnp.maximum(ls[0], ls[1]), ls[2])
    ws = [jnp.exp(l - mx) for l in ls]
    den = ws[0] + ws[1] + ws[2]
    dil = (ws[0] * os_[0] + ws[1] * os_[1] + ws[2] * os_[2]) / den
    nf = fox_ref.shape[1]
    mix = _dot(fox_ref[...], w_ref[:nf, :]) + _dot(dil.astype(BF16), w_ref[nf:, :])
    out_ref[...] = x_ref[...] + gate_ref[0] * mix


def out_proj_even(fox, dil_o, dil_l, w, x2, gate, seq, tm=512):
    n_tok, d = x2.shape
    tm = min(tm, seq)
    nf = fox.shape[1]
    nd = N_DIL_SLOTS * HEAD_DIM
    row = lambda i: (i, 0)
    return pl.pallas_call(
        _oproj0_kernel,
        out_shape=jax.ShapeDtypeStruct((n_tok, d), F32),
        grid=(n_tok // tm,),
        in_specs=[pl.BlockSpec((tm, nf), row)]
                 + [pl.BlockSpec((tm, nd), row)] * 6
                 + [pl.BlockSpec((nf + nd, d), lambda i: (0, 0)),
                    pl.BlockSpec((tm, d), row),
                    pl.BlockSpec((1, 1, d), lambda i: (i * tm // seq, 0, 0))],
        out_specs=pl.BlockSpec((tm, d), row),
        compiler_params=_cparams(("arbitrary",), VMEM_LIMIT_BYTES),
        name="out_proj_even",
    )(fox, *dil_o, *dil_l, w, x2, gate.reshape(-1, 1, d))


def _oproj_kernel(a_ref, w_ref, x_ref, gate_ref, out_ref):
    out_ref[...] = x_ref[...] + gate_ref[0] * _dot(a_ref[...], w_ref[...])


def out_proj_odd(a, w, x2, gate, seq, tm=512):
    n_tok, d = x2.shape
    tm = min(tm, seq)
    ka = a.shape[1]
    row = lambda i: (i, 0)
    return pl.pallas_call(
        _oproj_kernel,
        out_shape=jax.ShapeDtypeStruct((n_tok, d), F32),
        grid=(n_tok // tm,),
        in_specs=[pl.BlockSpec((tm, ka), row),
                  pl.BlockSpec((ka, d), lambda i: (0, 0)),
                  pl.BlockSpec((tm, d), row),
                  pl.BlockSpec((1, 1, d), lambda i: (i * tm // seq, 0, 0))],
        out_specs=pl.BlockSpec((tm, d), row),
        compiler_params=_cparams(("arbitrary",), VMEM_LIMIT_BYTES),
        name="out_proj_odd",
    )(a, w, x2, gate.reshape(-1, 1, d))


def _router_kernel(x_ref, g_ref, sc_ref, sh_ref, rw_ref, rb_ref,
                   h_ref, ti_ref, tg_ref, rk_ref, cnt_ref, carry):
    i = pl.program_id(0)

    @pl.when(i == 0)
    def _():
        carry[...] = jnp.zeros_like(carry)

    tm = x_ref.shape[0]
    x = x_ref[...]
    ms = jnp.mean(x * x, axis=-1, keepdims=True)
    y = x * lax.rsqrt(ms + EPS) * g_ref[...]
    h = (y * (1.0 + sc_ref[0]) + sh_ref[0]).astype(BF16)
    h_ref[...] = h
    logits = _dot_f32(h.astype(F32), rw_ref[...]) + rb_ref[...]
    lane = lax.broadcasted_iota(I32, (tm, LANES), 1)
    l = logits
    vals, idxs, hots = [], [], []
    for _ in range(TOP_K):
        mx = jnp.max(l, axis=1, keepdims=True)
        idx = jnp.min(jnp.where(l == mx, lane, LANES), axis=1, keepdims=True)
        hot = lane == idx
        l = jnp.where(hot, -jnp.inf, l)
        vals.append(mx)
        idxs.append(idx)
        hots.append(hot)
    es = [jnp.exp(v - vals[0]) for v in vals]
    den = es[0] + es[1] + es[2] + es[3]
    member = jnp.zeros((tm, LANES), F32)
    for hot in hots:
        member = member + jnp.where(hot, 1.0, 0.0)
    row = lax.broadcasted_iota(I32, (tm, tm), 0)
    col = lax.broadcasted_iota(I32, (tm, tm), 1)
    tri = jnp.where(row > col, 1.0, 0.0).astype(BF16)
    before = _dot(tri, member.astype(BF16)) + carry[...]
    carry[...] = carry[...] + jnp.sum(member, axis=0, keepdims=True)
    ti = jnp.zeros((tm, LANES), I32)
    tg = jnp.zeros((tm, LANES), F32)
    rk = jnp.zeros((tm, LANES), I32)
    for k in range(TOP_K):
        rank = jnp.sum(jnp.where(hots[k], before, 0.0), axis=1, keepdims=True).astype(I32)
        ti = jnp.where(lane == k, idxs[k], ti)
        tg = jnp.where(lane == k, es[k] / den, tg)
        rk = jnp.where(lane == k, rank, rk)
    ti_ref[...] = ti
    tg_ref[...] = tg
    rk_ref[...] = rk
    cnt_ref[...] = jnp.broadcast_to(carry[...], cnt_ref.shape)


def moe_router(x2, g, scale, shift, rw, rb, seq, tm=256):
    n_tok, d = x2.shape
    tm = min(tm, seq)
    rwp = jnp.zeros((d, LANES), F32).at[:, :N_EXPERTS].set(rw)
    rbp = jnp.full((1, LANES), NEG, F32).at[0, :N_EXPERTS].set(rb)
    row = lambda i: (i, 0)
    bsel = lambda i: (i * tm // seq, 0, 0)
    return pl.pallas_call(
        _router_kernel,
        out_shape=[jax.ShapeDtypeStruct((n_tok, d), BF16),
                   jax.ShapeDtypeStruct((n_tok, LANES), I32),
                   jax.ShapeDtypeStruct((n_tok, LANES), F32),
                   jax.ShapeDtypeStruct((n_tok, LANES), I32),
                   jax.ShapeDtypeStruct((8, LANES), F32)],
        grid=(n_tok // tm,),
        in_specs=[pl.BlockSpec((tm, d), row),
                  pl.BlockSpec((1, d), lambda i: (0, 0)),
                  pl.BlockSpec((1, 1, d), bsel),
                  pl.BlockSpec((1, 1, d), bsel),
                  pl.BlockSpec((d, LANES), lambda i: (0, 0)),
                  pl.BlockSpec((1, LANES), lambda i: (0, 0))],
        out_specs=[pl.BlockSpec((tm, d), row),
                   pl.BlockSpec((tm, LANES), row),
                   pl.BlockSpec((tm, LANES), row),
                   pl.BlockSpec((tm, LANES), row),
                   pl.BlockSpec((8, LANES), lambda i: (0, 0))],
        scratch_shapes=[pltpu.VMEM((1, LANES), F32)],
        compiler_params=_cparams(("arbitrary",)),
        name="moe_router",
    )(x2, g.reshape(1, d), scale.reshape(-1, 1, d), shift.reshape(-1, 1, d), rwp, rbp)


def _dispatch_kernel(dest_ref, h_ref, xs_in_ref, xs_ref, sem):
    del xs_in_ref
    tm = h_ref.shape[0]

    def row_copy(i, d):
        return pltpu.make_async_copy(h_ref.at[pl.ds(i, 1)], xs_ref.at[pl.ds(d, 1)], sem)

    def issue(i, c):
        for k in range(TOP_K):
            row_copy(i, dest_ref[i * TOP_K + k]).start()
        return c

    lax.fori_loop(0, tm, issue, 0)

    def drain(i, c):
        for k in range(TOP_K):
            row_copy(0, 0).wait()
        return c

    lax.fori_loop(0, tm, drain, 0)


def moe_dispatch(h32, dest_flat, rows, tm=256):
    n_tok, dw = h32.shape
    tm = min(tm, n_tok)
    xs0 = jnp.zeros((rows, dw), jnp.uint32)
    return pl.pallas_call(
        _dispatch_kernel,
        out_shape=jax.ShapeDtypeStruct((rows, dw), jnp.uint32),
        grid=(n_tok // tm,),
        in_specs=[pl.BlockSpec((tm * TOP_K,), lambda i: (i,), memory_space=pltpu.SMEM),
                  pl.BlockSpec((tm, dw), lambda i: (i, 0)),
                  pl.BlockSpec(memory_space=pl.ANY)],
        out_specs=pl.BlockSpec(memory_space=pl.ANY),
        scratch_shapes=[pltpu.SemaphoreType.DMA(())],
        input_output_aliases={2: 0},
        compiler_params=_cparams(("arbitrary",)),
        name="moe_dispatch",
    )(dest_flat, h32, xs0)


def _experts_kernel(ce_ref, nu_ref, x_ref, w1_ref, b1_ref, w2_ref, b2_ref, y_ref, act_scr, *, tn):
    c = pl.program_id(0)
    de = act_scr.shape[1]

    @pl.when(c < nu_ref[0])
    def _():
        x = x_ref[...]
        for j in range(de // tn):
            sl = slice(j * tn, (j + 1) * tn)
            sl2 = slice(de + j * tn, de + (j + 1) * tn)
            glu = _dot(x, w1_ref[0, :, sl]) + b1_ref[0, :, sl]
            lin = _dot(x, w1_ref[0, :, sl2]) + b1_ref[0, :, sl2]
            glu = jnp.minimum(glu, SWIGLU_LIMIT)
            lin = jnp.clip(lin, -SWIGLU_LIMIT, SWIGLU_LIMIT)
            act_scr[:, sl] = (glu * jax.nn.sigmoid(SWIGLU_ALPHA * glu) * (lin + 1.0)).astype(BF16)
        a = act_scr[...]
        d = y_ref.shape[1]
        for j in range(d // tn):
            sl = slice(j * tn, (j + 1) * tn)
            y_ref[:, sl] = _dot(a, w2_ref[0, :, sl]) + b2_ref[0, :, sl]

    @pl.when(c >= nu_ref[0])
    def _():
        y_ref[...] = jnp.zeros_like(y_ref)


def moe_experts(xs, chunk_exp, n_used, w1, b1, w2, b2, tn=256):
    rows, d = xs.shape
    ne, _, de2 = w1.shape
    de = de2 // 2
    n_chunks = rows // MOE_CHUNK
    grid_spec = pltpu.PrefetchScalarGridSpec(
        num_scalar_prefetch=2,
        grid=(n_chunks,),
        in_specs=[pl.BlockSpec((MOE_CHUNK, d), lambda c, ce, nu: (c, 0)),
                  pl.BlockSpec((1, d, de2), lambda c, ce, nu: (ce[c], 0, 0)),
                  pl.BlockSpec((1, 1, de2), lambda c, ce, nu: (ce[c], 0, 0)),
                  pl.BlockSpec((1, de, d), lambda c, ce, nu: (ce[c], 0, 0)),
                  pl.BlockSpec((1, 1, d), lambda c, ce, nu: (ce[c], 0, 0))],
        out_specs=pl.BlockSpec((MOE_CHUNK, d), lambda c, ce, nu: (c, 0)),
        scratch_shapes=[pltpu.VMEM((MOE_CHUNK, de), BF16)],
    )
    return pl.pallas_call(
        functools.partial(_experts_kernel, tn=min(tn, de)),
        out_shape=jax.ShapeDtypeStruct((rows, d), F32),
        grid_spec=grid_spec,
        compiler_params=_cparams(("arbitrary",), VMEM_LIMIT_BYTES),
        name="moe_experts",
    )(chunk_exp, n_used, xs, w1, b1.reshape(ne, 1, de2), w2, b2.reshape(ne, 1, d))


def _combine_kernel(dest_ref, y_ref, tg_ref, x_ref, gate_ref, o_ref, buf, sem):
    tm = x_ref.shape[0]

    def row_copy(i, k, d):
        return pltpu.make_async_copy(y_ref.at[pl.ds(d, 1)], buf.at[k, pl.ds(i, 1)], sem)

    def issue(i, c):
        for k in range(TOP_K):
            row_copy(i, k, dest_ref[i * TOP_K + k]).start()
        return c

    lax.fori_loop(0, tm, issue, 0)

    def drain(i, c):
        for k in range(TOP_K):
            row_copy(0, 0, 0).wait()
        return c

    lax.fori_loop(0, tm, drain, 0)
    tg = tg_ref[...]
    moe = tg[:, 0:1] * buf[0]
    for k in range(1, TOP_K):
        moe = moe + tg[:, k:k + 1] * buf[k]
    o_ref[...] = x_ref[...] + gate_ref[0] * moe


def moe_combine(y, dest_flat, tg, x2, gate, seq, tm=256):
    n_tok, d = x2.shape
    tm = min(tm, seq)
    row = lambda i: (i, 0)
    return pl.pallas_call(
        _combine_kernel,
        out_shape=jax.ShapeDtypeStruct((n_tok, d), F32),
        grid=(n_tok // tm,),
        in_specs=[pl.BlockSpec((tm * TOP_K,), lambda i: (i,), memory_space=pltpu.SMEM),
                  pl.BlockSpec(memory_space=pl.ANY),
                  pl.BlockSpec((tm, LANES), row),
                  pl.BlockSpec((tm, d), row),
                  pl.BlockSpec((1, 1, d), lambda i: (i * tm // seq, 0, 0))],
        out_specs=pl.BlockSpec((tm, d), row),
        scratch_shapes=[pltpu.VMEM((TOP_K, tm, d), F32), pltpu.SemaphoreType.DMA(())],
        compiler_params=_cparams(("arbitrary",), VMEM_LIMIT_BYTES),
        name="moe_combine",
    )(dest_flat, y, tg, x2, gate.reshape(-1, 1, d))


def moe_block(x2, g, scale, shift, gate, rw, rb, w1, b1, w2, b2, seq):
    n_tok, d = x2.shape
    h, ti, tg, rk, cnt = moe_router(x2, g, scale, shift, rw, rb, seq)
    counts = cnt[0, :N_EXPERTS].astype(I32)
    padded = ((counts + MOE_CHUNK - 1) // MOE_CHUNK) * MOE_CHUNK
    ends_pad = jnp.cumsum(padded)
    start_pad = ends_pad - padded
    n_chunks = -(-(n_tok * TOP_K) // MOE_CHUNK) + N_EXPERTS
    rows = n_chunks * MOE_CHUNK
    dest = (start_pad[ti[:, :TOP_K]] + rk[:, :TOP_K]).reshape(-1)
    chunk_exp = jnp.minimum(
        jnp.searchsorted(ends_pad, jnp.arange(n_chunks, dtype=I32) * MOE_CHUNK, side='right'),
        N_EXPERTS - 1).astype(I32)
    n_used = (ends_pad[-1:] // MOE_CHUNK).astype(I32)
    h32 = lax.bitcast_convert_type(h.reshape(n_tok, d // 2, 2), jnp.uint32)
    xs32 = moe_dispatch(h32, dest, rows)
    xs = lax.bitcast_convert_type(xs32, BF16).reshape(rows, d)
    y = moe_experts(xs, chunk_exp, n_used, w1.astype(BF16), b1, w2.astype(BF16), b2)
    return moe_combine(y, dest, tg, x2, gate, seq)


def _dsa_index_kernel(qi_ref, k2_ref, wi_ref, mask_ref, ka_scr, kb_scr, key_scr, j_scr,
                      *, S, KC, n_sel, idx_scale):
    qb = pl.program_id(1)
    R = qi_ref.shape[1]
    nchunks = S // KC
    lane = lax.broadcasted_iota(I32, (1, LANES), 1)

    @pl.when(qb == 0)
    def _():
        kk = k2_ref[0]
        ka_scr[...] = jnp.where(lane < HEAD_DIM, kk, 0.0).astype(BF16)
        kb_scr[...] = jnp.where(lane >= HEAD_DIM, kk, 0.0).astype(BF16)

    nck = (qb * R + R + KC - 1) // KC
    w = wi_ref[0].astype(BF16).astype(F32)
    qv = qi_ref[0].astype(BF16)

    def relu_bf16(d):
        return jnp.maximum(d, 0.0).astype(BF16).astype(F32)

    row = qb * R + lax.broadcasted_iota(I32, (R, KC), 0)
    col0 = lax.broadcasted_iota(I32, (R, KC), 1)

    def score_chunk(c, carry):
        off = pl.multiple_of(c * KC, KC)
        ka = ka_scr[pl.ds(off, KC), :]
        kb = kb_scr[pl.ds(off, KC), :]
        sc = jnp.zeros((R, KC), F32)
        for pp in range(N_IDX_HEADS // 2):
            qp = qv[:, pp * LANES:(pp + 1) * LANES]
            sc = sc + w[:, 2 * pp:2 * pp + 1] * relu_bf16(_dot_nt(qp, ka))
            sc = sc + w[:, 2 * pp + 1:2 * pp + 2] * relu_bf16(_dot_nt(qp, kb))
        sc = sc * idx_scale
        sc = jnp.where(sc == 0.0, 0.0, sc)
        bits = pltpu.bitcast(sc, I32)
        key = jnp.where(bits < 0, bits ^ jnp.int32(0x7FFFFFFF), bits)
        key_scr[c] = jnp.where(off + col0 <= row, key, INT_MIN)
        return carry

    lax.fori_loop(0, nck, score_chunk, 0)

    def count(pred):
        def body(c, acc):
            kc = key_scr[c]
            for j in range(KC // LANES):
                acc = acc + jnp.where(pred(kc[:, j * LANES:(j + 1) * LANES], c * KC + j * LANES), 1.0, 0.0)
            return acc
        acc = lax.fori_loop(0, nck, body, jnp.zeros((R, LANES), F32))
        return jnp.broadcast_to(jnp.sum(acc, axis=1, keepdims=True), (R, LANES))

    nsel_f = float(n_sel)

    def bit_step(t, carry):
        cur, cnt_cur = carry
        cand = cur + (jnp.int32(1) << (31 - t))
        cnt = count(lambda kc, base: kc >= cand)
        ok = cnt >= nsel_f
        return jnp.where(ok, cand, cur), jnp.where(ok, cnt, cnt_cur)

    tau, cnt_tau = lax.fori_loop(
        0, 32, bit_step,
        (jnp.full((R, LANES), INT_MIN, I32), jnp.full((R, LANES), float(S), F32)))

    j_scr[...] = jnp.full((R, LANES), S, I32)
    overflow = jnp.logical_and(tau > INT_MIN, cnt_tau > nsel_f)
    any_over = jnp.max(jnp.where(overflow, 1.0, 0.0)) > 0.0
    lane_r = lax.broadcasted_iota(I32, (R, LANES), 1)

    @pl.when(any_over)
    def _():
        need = nsel_f - count(lambda kc, base: kc > tau)

        def jstep(t, cur):
            cand = cur + (jnp.int32(1) << (S.bit_length() - 2 - t))
            cnt = count(lambda kc, base: jnp.logical_and(kc == tau, base + lane_r < cand))
            return jnp.where(cnt < need, cand, cur)

        j_scr[...] = lax.fori_loop(0, S.bit_length() - 1, jstep, jnp.zeros((R, LANES), I32))

    jlim = j_scr[...]
    for c in range(nchunks):
        @pl.when(c < nck)
        def _():
            kc = key_scr[c]
            for j in range(KC // LANES):
                kj = kc[:, j * LANES:(j + 1) * LANES]
                pos = c * KC + j * LANES + lane_r
                keep = jnp.logical_or(kj > tau, jnp.logical_and(kj == tau, pos <= jlim))
                keep = jnp.logical_and(keep, kj > INT_MIN)
                mask_ref[0, :, c * KC + j * LANES:c * KC + (j + 1) * LANES] = (
                    jnp.where(keep, 1, 0).astype(jnp.int8))

        @pl.when(c >= nck)
        def _():
            mask_ref[0, :, c * KC:(c + 1) * KC] = jnp.zeros((R, KC), jnp.int8)


def dsa_index_mask(aux, qcol, kcol, wcol, n_sel, R=128, KC=512):
    bsz, seq, _ = aux.shape
    R = min(R, seq)
    KC = min(KC, seq)
    nq = N_IDX_HEADS * HEAD_DIM
    return pl.pallas_call(
        functools.partial(_dsa_index_kernel, S=seq, KC=KC, n_sel=n_sel,
                          idx_scale=float(nq ** -0.5)),
        out_shape=jax.ShapeDtypeStruct((bsz, seq, seq), jnp.int8),
        grid=(bsz, seq // R),
        in_specs=[pl.BlockSpec((1, R, nq), lambda b, q: (b, q, qcol // nq)),
                  pl.BlockSpec((1, seq, LANES), lambda b, q: (b, 0, kcol // LANES)),
                  pl.BlockSpec((1, R, LANES), lambda b, q: (b, q, wcol // LANES))],
        out_specs=pl.BlockSpec((1, R, seq), lambda b, q: (b, q, 0)),
        scratch_shapes=[pltpu.VMEM((seq, LANES), BF16), pltpu.VMEM((seq, LANES), BF16),
                        pltpu.VMEM((seq // KC, R, KC), I32), pltpu.VMEM((R, LANES), I32)],
        compiler_params=_cparams(("arbitrary", "arbitrary"), VMEM_LIMIT_BYTES),
        name="dsa_index_mask",
    )(aux, aux, aux)


def _dsa_attn_kernel(qi_ref, ki_ref, q_ref, k_ref, v_ref, mask_ref, tz_ref, o_ref,
                     m_scr, l_scr, acc_scr, *, TQ, TK, DCL):
    s = pl.program_id(1)
    qi = qi_ref[s]
    ki = ki_ref[s]
    k_last = ((qi + 1) * TQ - 1) // TK

    @pl.when(ki == 0)
    def _():
        m_scr[...] = jnp.full_like(m_scr, NEG)
        l_scr[...] = jnp.zeros_like(l_scr)
        acc_scr[...] = jnp.zeros_like(acc_scr)

    lane = lax.broadcasted_iota(I32, (1, LANES), 1)
    lo_half = lane < HEAD_DIM
    d0 = qi * TQ - ki * TK
    cb0 = (DCL - jnp.minimum(d0, DCL)) // LANES
    madd = jnp.where(mask_ref[0].astype(I32) != 0, 0.0, NEG)
    npair = q_ref.shape[2] // LANES
    for pp in range(npair):
        cs = slice(pp * LANES, (pp + 1) * LANES)
        q = q_ref[0, :, cs] * QK_SCALE
        k = k_ref[0, :, cs]
        v = v_ref[0, :, cs]
        pvs, alphas = [], []
        for hh in range(2):
            h = 2 * pp + hh
            qsel = lo_half if hh == 0 else jnp.logical_not(lo_half)
            bias = jnp.concatenate([tz_ref[h, cb0 + jj] for jj in range(TK // LANES)], axis=1)
            sc = _dot_nt(_keep_lanes(q, qsel), k) + bias + madd
            m_prev = m_scr[h]
            m_new = jnp.maximum(m_prev, jnp.max(sc, axis=1, keepdims=True))
            alpha = jnp.exp(m_prev - m_new)
            pm = jnp.exp(sc - _lane_tile(m_new, TK // LANES))
            l_scr[h] = alpha * l_scr[h] + jnp.sum(pm, axis=1, keepdims=True)
            m_scr[h] = m_new
            pvs.append(_dot(pm.astype(BF16), v))
            alphas.append(alpha)
        acc_scr[:, cs] = (acc_scr[:, cs] * jnp.where(lo_half, alphas[0], alphas[1])
                          + jnp.where(lo_half, pvs[0], pvs[1]))

    @pl.when(ki == k_last)
    def _():
        for pp in range(npair):
            cs = slice(pp * LANES, (pp + 1) * LANES)
            l = jnp.where(lo_half, l_scr[2 * pp], l_scr[2 * pp + 1])
            o_ref[0, :, cs] = (acc_scr[:, cs] / l).astype(o_ref.dtype)


def _dsa_bias_table(rel_bias, seq, TQ, TK, DCL):
    bvec = rel_bias[_t5_bucket(jnp.arange(seq))].astype(F32)
    i = np.arange(TQ)[:, None]
    c = np.arange(DCL + TK)[None, :]
    d = np.clip(i - c + DCL, 0, seq - 1)
    tz = bvec[d]
    tz = tz.transpose(2, 0, 1).reshape(bvec.shape[1], TQ, (DCL + TK) // LANES, LANES)
    return tz.transpose(0, 2, 1, 3)


def _np_t5_bucket(d):
    max_exact = N_BUCKETS // 2
    df = np.maximum(d, 1).astype(np.float32)
    large = max_exact + (np.log(df / max_exact) / math.log(MAX_DISTANCE / max_exact)
                         * (N_BUCKETS - max_exact)).astype(np.int32)
    return np.where(d < max_exact, d, np.minimum(large, N_BUCKETS - 1))


def dsa_attention(q_src, qcol, kv, mask, rel_bias, TQ=128, TK=256, dcl=1792):
    bsz, seq, _ = q_src.shape
    TQ = min(TQ, seq)
    TK = min(TK, seq)
    hw = N_DSA_HEADS * HEAD_DIM
    if seq > dcl:
        DCL = dcl
        far = _np_t5_bucket(np.arange(DCL - TK + 1, seq))
        assert (far == far[0]).all()
    else:
        DCL = seq
    tz = _dsa_bias_table(rel_bias, seq, TQ, TK, DCL)
    qs, ks = _tri_schedule(seq // TQ, TK // TQ)
    grid_spec = pltpu.PrefetchScalarGridSpec(
        num_scalar_prefetch=2,
        grid=(bsz, qs.shape[0]),
        in_specs=[
            pl.BlockSpec((1, TQ, hw), lambda b, s, qi, ki: (b, qi[s], qcol // hw)),
            pl.BlockSpec((1, TK, hw), lambda b, s, qi, ki: (b, ki[s], 0)),
            pl.BlockSpec((1, TK, hw), lambda b, s, qi, ki: (b, ki[s], 1)),
            pl.BlockSpec((1, TQ, TK), lambda b, s, qi, ki: (b, qi[s], ki[s])),
            pl.BlockSpec(tz.shape, lambda b, s, qi, ki: (0, 0, 0, 0)),
        ],
        out_specs=pl.BlockSpec((1, TQ, hw), lambda b, s, qi, ki: (b, qi[s], 0)),
        scratch_shapes=[pltpu.VMEM((N_DSA_HEADS, TQ, LANES), F32),
                        pltpu.VMEM((N_DSA_HEADS, TQ, LANES), F32),
                        pltpu.VMEM((TQ, hw), F32)],
    )
    return pl.pallas_call(
        functools.partial(_dsa_attn_kernel, TQ=TQ, TK=TK, DCL=DCL),
        out_shape=jax.ShapeDtypeStruct((bsz, seq, hw), BF16),
        grid_spec=grid_spec,
        compiler_params=_cparams(("arbitrary", "arbitrary"), VMEM_LIMIT_BYTES),
        name="dsa_attention",
    )(qs, ks, q_src, kv, kv, mask, tz)


def _rms_kernel(x_ref, g_ref, o_ref):
    x = x_ref[...]
    ms = jnp.mean(x * x, axis=-1, keepdims=True)
    o_ref[...] = x * lax.rsqrt(ms + EPS) * g_ref[...]


def final_rmsnorm(x2, g, tm=512):
    n_tok, d = x2.shape
    tm = min(tm, n_tok)
    return pl.pallas_call(
        _rms_kernel,
        out_shape=jax.ShapeDtypeStruct((n_tok, d), F32),
        grid=(n_tok // tm,),
        in_specs=[pl.BlockSpec((tm, d), lambda i: (i, 0)), pl.BlockSpec((1, d), lambda i: (0, 0))],
        out_specs=pl.BlockSpec((tm, d), lambda i: (i, 0)),
        compiler_params=_cparams(("arbitrary",)),
        name="final_rmsnorm",
    )(x2, g.reshape(1, d))


def even_mixer_residual(x2, shift1, scale1, gate1, norm1, w_in, fox_fb, w_out, rel_bias, bsz, seq):
    d = x2.shape[1]
    nfq = 3 * N_FOX_HEADS * HEAD_DIM
    w_main = jnp.concatenate([w_in[:, :nfq], w_in[:, nfq + N_FOX_HEADS:]], axis=1).astype(BF16)
    rep = np.repeat(np.arange(N_FOX_HEADS), FOX_PARTS)
    gcols = np.concatenate([rep, rep])
    w_gate = jnp.zeros((d, LANES), F32).at[:, :2 * FOX_XW].set(w_in[:, nfq + gcols])
    fb = jnp.zeros((1, LANES), F32).at[0, :2 * FOX_XW].set(fox_fb[gcols])
    proj, gate_z = norm_mod_matmul(x2, norm1, scale1, shift1, w_main, w_gate, seq)
    c = proj.shape[1]
    proj3 = proj.reshape(bsz, seq, c)
    qx, kx = fox_gate_columns(gate_z.reshape(bsz, seq, LANES), fb)
    fox = fox_attention(proj3, qx, kx, N_FOX_HEADS)
    dil_o, dil_l = [], []
    for g, (window, dil) in enumerate(DIL_PAIRS):
        bias = _dil_bias_tiles(rel_bias, g, window, dil)
        o, l = dilated_group_attention(proj3, bias, g, dil, nfq + g * N_DIL_SLOTS * HEAD_DIM)
        dil_o.append(o)
        dil_l.append(l)
    return out_proj_even(fox.reshape(bsz * seq, -1), dil_o, dil_l, w_out.astype(BF16), x2, gate1, seq)


def odd_mixer_residual(x2, shift1, scale1, gate1, norm1, w_in, kv_norm, w_ukv, w_out, rel_bias, bsz, seq):
    d = x2.shape[1]
    nq = N_DSA_HEADS * HEAD_DIM
    ni = N_IDX_HEADS * HEAD_DIM
    c_ckv, c_qi = nq, nq + KV_RANK
    c_ki = c_qi + ni
    c_wi = c_ki + HEAD_DIM
    w_main = w_in[:, :nq].astype(BF16)
    c_w, c_k, c_q = KV_RANK, KV_RANK + LANES, KV_RANK + 2 * LANES
    w_aux = jnp.zeros((d, c_q + ni), F32)
    w_aux = w_aux.at[:, :KV_RANK].set(w_in[:, c_ckv:c_qi])
    w_aux = w_aux.at[:, c_w:c_w + N_IDX_HEADS].set(w_in[:, c_wi:])
    w_aux = w_aux.at[:, c_k:c_k + HEAD_DIM].set(w_in[:, c_ki:c_wi])
    w_aux = w_aux.at[:, c_k + HEAD_DIM:c_q].set(w_in[:, c_ki:c_wi])
    w_aux = w_aux.at[:, c_q:].set(w_in[:, c_qi:c_ki])
    proj, aux = norm_mod_matmul(x2, norm1, scale1, shift1, w_main, w_aux, seq)
    zeros = jnp.zeros((bsz, KV_RANK), F32)
    kv = norm_mod_matmul(aux[:, :KV_RANK], kv_norm, zeros, zeros, w_ukv.astype(BF16), None, seq)
    proj3 = proj.reshape(bsz, seq, -1)
    n_sel = min(DSA_TOPK, seq // 4)
    mask = dsa_index_mask(aux.reshape(bsz, seq, -1), c_q, c_k, c_w, n_sel)
    att = dsa_attention(proj3, 0, kv.reshape(bsz, seq, -1), mask, rel_bias)
    return out_proj_odd(att.reshape(bsz * seq, -1), w_out.astype(BF16), x2, gate1, seq)


def kernel(x, c, rel_bias, l0_norm1, l0_ada_w, l0_ada_b, l0_w_in, l0_fox_fb, l0_w_out, l0_norm2, l0_router_w, l0_router_b, l0_w1, l0_b1, l0_w2, l0_b2, l1_norm1, l1_ada_w, l1_ada_b, l1_w_in, l1_kv_norm, l1_w_ukv, l1_w_out, l1_norm2, l1_router_w, l1_router_b, l1_w1, l1_b1, l1_w2, l1_b2, final_norm):
    bsz, seq, d = x.shape
    x2 = x.reshape(bsz * seq, d)
    layers = (
        (l0_norm1, l0_ada_w, l0_ada_b, l0_norm2, l0_router_w, l0_router_b, l0_w1, l0_b1, l0_w2, l0_b2),
        (l1_norm1, l1_ada_w, l1_ada_b, l1_norm2, l1_router_w, l1_router_b, l1_w1, l1_b1, l1_w2, l1_b2),
    )
    for i, (norm1, ada_w, ada_b, norm2, rw, rb, w1, b1, w2, b2) in enumerate(layers):
        mods = ada_mods(c, ada_w, ada_b)
        shift1, scale1, gate1, shift2, scale2, gate2 = (mods[:, j * d:(j + 1) * d] for j in range(6))
        if i % 2 == 0:
            x2 = even_mixer_residual(x2, shift1, scale1, gate1, norm1, l0_w_in, l0_fox_fb, l0_w_out,
                                     rel_bias, bsz, seq)
        else:
            x2 = odd_mixer_residual(x2, shift1, scale1, gate1, norm1, l1_w_in, l1_kv_norm, l1_w_ukv,
                                    l1_w_out, rel_bias, bsz, seq)
        x2 = moe_block(x2, norm2, scale2, shift2, gate2, rw, rb, w1, b1, w2, b2, seq)
    return final_rmsnorm(x2, final_norm).reshape(bsz, seq, d)
```

```python
import functools
import math

import numpy as np
import jax
import jax.numpy as jnp
from jax import lax
from jax.experimental import pallas as pl
from jax.experimental.pallas import tpu as pltpu

F32, BF16, I32 = jnp.float32, jnp.bfloat16, jnp.int32

LANES = 128
VMEM_LIMIT_BYTES = 56 * 1024 * 1024

HEAD_DIM = 64
N_FOX_HEADS = 8
DIL_PAIRS = ((128, 1), (512, 4), (2048, 16))
N_DIL_SLOTS = 4
N_DSA_HEADS = 16
KV_RANK = 256
N_IDX_HEADS = 8
DSA_TOPK = 256
N_BUCKETS = 32
MAX_DISTANCE = 2048
N_EXPERTS = 32
TOP_K = 4
D_EXPERT = 1024
SWIGLU_ALPHA = 1.702
SWIGLU_LIMIT = 7.0
MOE_CHUNK = 256
EPS = 1e-6

NEG = -1e30
INT_MIN = -(2 ** 31)
QK_SCALE = HEAD_DIM ** -0.5
FOX_PARTS = 3
FOX_XW = FOX_PARTS * N_FOX_HEADS


def _cparams(sem, vmem=None):
    return pltpu.CompilerParams(dimension_semantics=sem, vmem_limit_bytes=vmem)


def _dot_nt(a, b):
    return lax.dot_general(a, b, (((1,), (1,)), ((), ())), preferred_element_type=F32)


def _dot(a, b):
    return jnp.dot(a, b, preferred_element_type=F32)


def _dot_f32(a, b):
    return jnp.dot(a, b, precision=lax.Precision.HIGHEST, preferred_element_type=F32)


def _keep_lanes(x, sel):
    return jnp.where(sel, x.astype(F32), 0.0).astype(x.dtype)


def _online_softmax(sc, m_prev, l_prev):
    cols = [sc[:, j:j + LANES] for j in range(0, sc.shape[1], LANES)]
    m_cur = functools.reduce(jnp.maximum, cols)
    m_new = jnp.maximum(m_prev, jnp.max(m_cur, axis=1, keepdims=True))
    alpha = jnp.exp(m_prev - m_new)
    ps = [jnp.exp(c - m_new) for c in cols]
    l_new = alpha * l_prev + jnp.sum(functools.reduce(jnp.add, ps), axis=1, keepdims=True)
    return m_new, l_new, alpha, jnp.concatenate([p.astype(BF16) for p in ps], axis=1)


def _tri_schedule(nq, kq_ratio=1):
    qs, ks = [], []
    for qi in range(nq):
        for ki in range(qi // kq_ratio + 1):
            qs.append(qi)
            ks.append(ki)
    return jnp.asarray(qs, I32), jnp.asarray(ks, I32)


def _ada_kernel(c_ref, w_ref, b_ref, o_ref):
    c = c_ref[...]
    sc = c * jax.nn.sigmoid(c)
    o_ref[...] = _dot_f32(sc, w_ref[...]) + b_ref[...]


def ada_mods(c, w, b):
    bsz, d = c.shape
    n = w.shape[1]
    tn = min(n, 1024)
    cp = jnp.zeros((8, d), F32).at[:bsz].set(c)
    out = pl.pallas_call(
        _ada_kernel,
        out_shape=jax.ShapeDtypeStruct((8, n), F32),
        grid=(n // tn,),
        in_specs=[pl.BlockSpec((8, d), lambda j: (0, 0)),
                  pl.BlockSpec((d, tn), lambda j: (0, j)),
                  pl.BlockSpec((1, tn), lambda j: (0, j))],
        out_specs=pl.BlockSpec((8, tn), lambda j: (0, j)),
        compiler_params=_cparams(("arbitrary",)),
        name="ada_mods",
    )(cp, w, b.reshape(1, n))
    return out[:bsz]


def _nmm_kernel(x_ref, g_ref, sc_ref, sh_ref, w_ref, *rest, tn, has_aux):
    if has_aux:
        wa_ref, o_ref, oa_ref = rest
    else:
        (o_ref,) = rest
    x = x_ref[...]
    ms = jnp.mean(x * x, axis=-1, keepdims=True)
    y = x * lax.rsqrt(ms + EPS) * g_ref[...]
    hf = y * (1.0 + sc_ref[0]) + sh_ref[0]
    h = hf.astype(BF16)
    n = o_ref.shape[1]
    for j0 in range(0, n, tn):
        j1 = min(j0 + tn, n)
        o_ref[:, j0:j1] = _dot(h, w_ref[:, j0:j1]).astype(o_ref.dtype)
    if has_aux:
        na = oa_ref.shape[1]
        for j0 in range(0, na, tn):
            j1 = min(j0 + tn, na)
            oa_ref[:, j0:j1] = _dot_f32(hf, wa_ref[:, j0:j1])


def norm_mod_matmul(x2, g, scale, shift, w, w_aux, seq, tm=512, tn=256):
    n_tok, din = x2.shape
    nout = w.shape[1]
    tm = min(tm, seq)
    has_aux = w_aux is not None
    bsel = lambda i: (i * tm // seq, 0, 0)
    in_specs = [pl.BlockSpec((tm, din), lambda i: (i, 0)),
                pl.BlockSpec((1, din), lambda i: (0, 0)),
                pl.BlockSpec((1, 1, din), bsel),
                pl.BlockSpec((1, 1, din), bsel),
                pl.BlockSpec((din, nout), lambda i: (0, 0))]
    out_shape = [jax.ShapeDtypeStruct((n_tok, nout), BF16)]
    out_specs = [pl.BlockSpec((tm, nout), lambda i: (i, 0))]
    args = [x2, g.reshape(1, din), scale.reshape(-1, 1, din), shift.reshape(-1, 1, din), w]
    if has_aux:
        na = w_aux.shape[1]
        in_specs.append(pl.BlockSpec((din, na), lambda i: (0, 0)))
        out_shape.append(jax.ShapeDtypeStruct((n_tok, na), F32))
        out_specs.append(pl.BlockSpec((tm, na), lambda i: (i, 0)))
        args.append(w_aux)
    outs = pl.pallas_call(
        functools.partial(_nmm_kernel, tn=min(tn, nout), has_aux=has_aux),
        out_shape=out_shape,
        grid=(n_tok // tm,),
        in_specs=in_specs,
        out_specs=out_specs,
        compiler_params=_cparams(("arbitrary",), VMEM_LIMIT_BYTES),
        name="norm_mod_matmul",
    )(*args)
    return outs if has_aux else outs[0]


def _foxcum_kernel(z_ref, fb_ref, qx_ref, kx_ref, carry):
    t = pl.program_id(1)

    @pl.when(t == 0)
    def _():
        carry[...] = jnp.zeros_like(carry)

    ts = z_ref.shape[1]
    z = z_ref[0] + fb_ref[...]
    lf = jnp.minimum(z, 0.0) - jnp.log1p(jnp.exp(-jnp.abs(z)))
    row = lax.broadcasted_iota(I32, (ts, ts), 0)
    col = lax.broadcasted_iota(I32, (ts, ts), 1)
    tri = jnp.where(row >= col, 1.0, 0.0).astype(F32)
    c = jnp.dot(tri, lf, precision=lax.Precision.HIGHEST, preferred_element_type=F32) + carry[...]
    carry[...] = c[ts - 1:ts, :]
    hi = c.astype(BF16).astype(F32)
    r1 = c - hi
    mid = r1.astype(BF16).astype(F32)
    lo = (r1 - mid).astype(BF16).astype(F32)
    lane = lax.broadcasted_iota(I32, (ts, LANES), 1)
    ph = lane % FOX_PARTS
    part = jnp.where(ph == 0, hi, jnp.where(ph == 1, mid, lo))
    first = lane < FOX_XW
    second = (lane >= FOX_XW) & (lane < 2 * FOX_XW)
    qx_ref[0] = jnp.where(first, part, jnp.where(second, 1.0, 0.0)).astype(BF16)
    kx_ref[0] = jnp.where(first, 1.0, jnp.where(second, -part, 0.0)).astype(BF16)


def fox_gate_columns(z, fb, ts=512):
    bsz, seq, _ = z.shape
    ts = min(ts, seq)
    return pl.pallas_call(
        _foxcum_kernel,
        out_shape=[jax.ShapeDtypeStruct((bsz, seq, LANES), BF16)] * 2,
        grid=(bsz, seq // ts),
        in_specs=[pl.BlockSpec((1, ts, LANES), lambda b, t: (b, t, 0)),
                  pl.BlockSpec((1, LANES), lambda b, t: (0, 0))],
        out_specs=[pl.BlockSpec((1, ts, LANES), lambda b, t: (b, t, 0))] * 2,
        scratch_shapes=[pltpu.VMEM((1, LANES), F32)],
        compiler_params=_cparams(("arbitrary", "arbitrary")),
        name="fox_gate_columns",
    )(z, fb)


def _fox_kernel(qi_ref, ki_ref, q_ref, k_ref, v_ref, qx_ref, kx_ref, o_ref, m_scr, l_scr, acc_scr, *, T):
    p = pl.program_id(1)
    s = pl.program_id(2)
    qi = qi_ref[s]
    ki = ki_ref[s]

    @pl.when(ki == 0)
    def _():
        m_scr[...] = jnp.full_like(m_scr, NEG)
        l_scr[...] = jnp.zeros_like(l_scr)
        acc_scr[...] = jnp.zeros_like(acc_scr)

    lane = lax.broadcasted_iota(I32, (1, LANES), 1)
    lo_half = lane < HEAD_DIM

    def step(masked):
        q = q_ref[0] * QK_SCALE
        qx = qx_ref[0]
        kcat = jnp.concatenate([k_ref[0], kx_ref[0]], axis=1)
        v = v_ref[0]
        if masked:
            row = lax.broadcasted_iota(I32, (T, T), 0)
            col = lax.broadcasted_iota(I32, (T, T), 1)
            causal = col <= row
        pvs, alphas = [], []
        for hh in range(2):
            xl = FOX_PARTS * (2 * p + hh)
            qsel = lo_half if hh == 0 else jnp.logical_not(lo_half)
            xsel = ((lane >= xl) & (lane < xl + FOX_PARTS)) | (
                (lane >= FOX_XW + xl) & (lane < FOX_XW + xl + FOX_PARTS))
            qcat = jnp.concatenate([_keep_lanes(q, qsel), _keep_lanes(qx, xsel)], axis=1)
            sc = _dot_nt(qcat, kcat)
            if masked:
                sc = jnp.where(causal, sc, NEG)
            m_scr[hh], l_scr[hh], alpha, pm = _online_softmax(sc, m_scr[hh], l_scr[hh])
            pvs.append(_dot(pm, v))
            alphas.append(alpha)
        acc_scr[...] = (acc_scr[...] * jnp.where(lo_half, alphas[0], alphas[1])
                        + jnp.where(lo_half, pvs[0], pvs[1]))

    @pl.when(ki < qi)
    def _():
        step(False)

    @pl.when(ki == qi)
    def _():
        step(True)
        l = jnp.where(lo_half, l_scr[0], l_scr[1])
        o_ref[0] = (acc_scr[...] / l).astype(o_ref.dtype)


def fox_attention(proj, qx, kx, n_heads, T=512):
    bsz, seq, _ = proj.shape
    T = min(T, seq)
    npair = n_heads // 2
    qs, ks = _tri_schedule(seq // T)
    grid_spec = pltpu.PrefetchScalarGridSpec(
        num_scalar_prefetch=2,
        grid=(bsz, npair, qs.shape[0]),
        in_specs=[
            pl.BlockSpec((1, T, LANES), lambda b, p, s, qi, ki: (b, qi[s], p)),
            pl.BlockSpec((1, T, LANES), lambda b, p, s, qi, ki: (b, ki[s], npair + p)),
            pl.BlockSpec((1, T, LANES), lambda b, p, s, qi, ki: (b, ki[s], 2 * npair + p)),
            pl.BlockSpec((1, T, LANES), lambda b, p, s, qi, ki: (b, qi[s], 0)),
            pl.BlockSpec((1, T, LANES), lambda b, p, s, qi, ki: (b, ki[s], 0)),
        ],
        out_specs=pl.BlockSpec((1, T, LANES), lambda b, p, s, qi, ki: (b, qi[s], p)),
        scratch_shapes=[pltpu.VMEM((2, T, LANES), F32), pltpu.VMEM((2, T, LANES), F32),
                        pltpu.VMEM((T, LANES), F32)],
    )
    return pl.pallas_call(
        functools.partial(_fox_kernel, T=T),
        out_shape=jax.ShapeDtypeStruct((bsz, seq, npair * LANES), BF16),
        grid_spec=grid_spec,
        compiler_params=_cparams(("arbitrary", "arbitrary", "arbitrary"), VMEM_LIMIT_BYTES),
        name="fox_attention",
    )(qs, ks, proj, proj, proj, qx, kx)


def _dil_kernel(q_ref, kp_ref, kc_ref, vp_ref, vc_ref, bias_ref, o_ref, lse_ref, *, W):
    ut = pl.program_id(3)
    lane = lax.broadcasted_iota(I32, (1, LANES), 1)
    lo_half = lane < HEAD_DIM
    q = q_ref[0] * QK_SCALE
    k = jnp.concatenate([kp_ref[0], kc_ref[0]], axis=0)
    v = jnp.concatenate([vp_ref[0], vc_ref[0]], axis=0)
    col = lax.broadcasted_iota(I32, (W, 2 * W), 1)
    no_prev = jnp.logical_and(ut == 0, col < W)
    outs, lses = [], []
    for hh in range(2):
        qsel = lo_half if hh == 0 else jnp.logical_not(lo_half)
        sc = _dot_nt(_keep_lanes(q, qsel), k) + bias_ref[0, hh]
        sc = jnp.where(no_prev, NEG, sc)
        mx = jnp.max(sc, axis=1, keepdims=True)
        e = jnp.exp(sc - mx)
        den = jnp.sum(e, axis=1, keepdims=True)
        outs.append(_dot((e / den).astype(BF16), v))
        lses.append(mx + jnp.log(den))
    o_ref[0] = jnp.where(lo_half, outs[0], outs[1])
    lse_ref[0] = jnp.where(lo_half, lses[0], lses[1])


def dilated_group_attention(proj, bias, g, dil, col0, W=128):
    bsz, seq, c = proj.shape
    sub = seq // dil
    pv = proj.reshape(bsz, sub, dil * c)
    cb = c // LANES
    qb = col0 // LANES
    kb = qb + 3 * N_DIL_SLOTS * HEAD_DIM // LANES
    vb = kb + 3 * N_DIL_SLOTS * HEAD_DIM // LANES
    prev = lambda u: jnp.maximum(u - 1, 0)
    out = pl.pallas_call(
        functools.partial(_dil_kernel, W=W),
        out_shape=[jax.ShapeDtypeStruct((bsz, sub, dil * 2 * LANES), F32)] * 2,
        grid=(bsz, dil, 2, sub // W),
        in_specs=[
            pl.BlockSpec((1, W, LANES), lambda b, a, p, u: (b, u, a * cb + qb + p)),
            pl.BlockSpec((1, W, LANES), lambda b, a, p, u: (b, prev(u), a * cb + kb + p)),
            pl.BlockSpec((1, W, LANES), lambda b, a, p, u: (b, u, a * cb + kb + p)),
            pl.BlockSpec((1, W, LANES), lambda b, a, p, u: (b, prev(u), a * cb + vb + p)),
            pl.BlockSpec((1, W, LANES), lambda b, a, p, u: (b, u, a * cb + vb + p)),
            pl.BlockSpec((1, 2, W, 2 * W), lambda b, a, p, u: (p, 0, 0, 0)),
        ],
        out_specs=[pl.BlockSpec((1, W, LANES), lambda b, a, p, u: (b, u, a * 2 + p))] * 2,
        compiler_params=_cparams(("arbitrary",) * 4),
        name=f"dilated_attention_g{g}",
    )(pv, pv, pv, pv, pv, bias)
    return [o.reshape(bsz * seq, 2 * LANES) for o in out]


def _t5_bucket(dist):
    max_exact = N_BUCKETS // 2
    d = jnp.maximum(dist, 0)
    df = jnp.maximum(d, 1).astype(F32)
    large = max_exact + (jnp.log(df / max_exact) / math.log(MAX_DISTANCE / max_exact)
                         * (N_BUCKETS - max_exact)).astype(I32)
    large = jnp.minimum(large, N_BUCKETS - 1)
    return jnp.where(d < max_exact, d, large)


def _bucket_bias(dist, table):
    onehot = (_t5_bucket(dist)[..., None] == jnp.arange(N_BUCKETS, dtype=I32)).astype(F32)
    out = jnp.einsum('...b,bh->h...', onehot, table.astype(F32), precision=lax.Precision.HIGHEST)
    return out


def _dil_bias_tiles(rel_bias, g, window, dil, W=128):
    assert window // dil == W
    i = np.arange(W)[:, None]
    j = np.arange(2 * W)[None, :]
    n = i + W - j
    valid = (n >= 0) & (n <= W)
    heads = rel_bias[:, g * N_DIL_SLOTS:(g + 1) * N_DIL_SLOTS]
    tiles = jnp.where(valid[None], _bucket_bias(jnp.asarray(np.clip(n, 0, W) * dil, I32), heads), NEG)
    return tiles.reshape(2, 2, W, 2 * W)


def _oproj0_kernel(fox_ref, o0_ref, o1_ref, o2_ref, l0_ref, l1_ref, l2_ref, w_ref, x_ref, gate_ref, out_ref):
    ls = [l0_ref[...], l1_ref[...], l2_ref[...]]
    os_ = [o0_ref[...], o1_ref[...], o2_ref[...]]
    mx = jnp.maximum(jnp.maximum(ls[0], ls[1]), ls[2])
    ws = [jnp.exp(l - mx) for l in ls]
    den = ws[0] + ws[1] + ws[2]
    dil = (ws[0] * os_[0] + ws[1] * os_[1] + ws[2] * os_[2]) / den
    nf = fox_ref.shape[1]
    mix = _dot(fox_ref[...], w_ref[:nf, :]) + _dot(dil.astype(BF16), w_ref[nf:, :])
    out_ref[...] = x_ref[...] + gate_ref[0] * mix


def out_proj_even(fox, dil_o, dil_l, w, x2, gate, seq, tm=512):
    n_tok, d = x2.shape
    tm = min(tm, seq)
    nf = fox.shape[1]
    nd = N_DIL_SLOTS * HEAD_DIM
    row = lambda i: (i, 0)
    return pl.pallas_call(
        _oproj0_kernel,
        out_shape=jax.ShapeDtypeStruct((n_tok, d), F32),
        grid=(n_tok // tm,),
        in_specs=[pl.BlockSpec((tm, nf), row)]
                 + [pl.BlockSpec((tm, nd), row)] * 6
                 + [pl.BlockSpec((nf + nd, d), lambda i: (0, 0)),
                    pl.BlockSpec((tm, d), row),
                    pl.BlockSpec((1, 1, d), lambda i: (i * tm // seq, 0, 0))],
        out_specs=pl.BlockSpec((tm, d), row),
        compiler_params=_cparams(("arbitrary",), VMEM_LIMIT_BYTES),
        name="out_proj_even",
    )(fox, *dil_o, *dil_l, w, x2, gate.reshape(-1, 1, d))


def _oproj_kernel(a_ref, w_ref, x_ref, gate_ref, out_ref):
    out_ref[...] = x_ref[...] + gate_ref[0] * _dot(a_ref[...], w_ref[...])


def out_proj_odd(a, w, x2, gate, seq, tm=512):
    n_tok, d = x2.shape
    tm = min(tm, seq)
    ka = a.shape[1]
    row = lambda i: (i, 0)
    return pl.pallas_call(
        _oproj_kernel,
        out_shape=jax.ShapeDtypeStruct((n_tok, d), F32),
        grid=(n_tok // tm,),
        in_specs=[pl.BlockSpec((tm, ka), row),
                  pl.BlockSpec((ka, d), lambda i: (0, 0)),
                  pl.BlockSpec((tm, d), row),
                  pl.BlockSpec((1, 1, d), lambda i: (i * tm // seq, 0, 0))],
        out_specs=pl.BlockSpec((tm, d), row),
        compiler_params=_cparams(("arbitrary",), VMEM_LIMIT_BYTES),
        name="out_proj_odd",
    )(a, w, x2, gate.reshape(-1, 1, d))


def _router_kernel(x_ref, g_ref, sc_ref, sh_ref, rw_ref, rb_ref,
                   h_ref, ti_ref, tg_ref, rk_ref, cnt_ref, carry):
    i = pl.program_id(0)

    @pl.when(i == 0)
    def _():
        carry[...] = jnp.zeros_like(carry)

    tm = x_ref.shape[0]
    x = x_ref[...]
    ms = jnp.mean(x * x, axis=-1, keepdims=True)
    y = x * lax.rsqrt(ms + EPS) * g_ref[...]
    hr = (y * (1.0 + sc_ref[0]) + sh_ref[0]).astype(BF16).astype(F32)
    bits = pltpu.bitcast(hr, I32)
    half = hr.shape[1] // 2
    h_ref[...] = lax.shift_right_logical(bits[:, :half], 16) | (bits[:, half:] & jnp.int32(-65536))
    logits = _dot_f32(hr, rw_ref[...]) + rb_ref[...]
    lane = lax.broadcasted_iota(I32, (tm, LANES), 1)
    l = logits
    vals, idxs, hots = [], [], []
    for _ in range(TOP_K):
        mx = jnp.max(l, axis=1, keepdims=True)
        idx = jnp.min(jnp.where(l == mx, lane, LANES), axis=1, keepdims=True)
        hot = lane == idx
        l = jnp.where(hot, -jnp.inf, l)
        vals.append(mx)
        idxs.append(idx)
        hots.append(hot)
    es = [jnp.exp(v - vals[0]) for v in vals]
    den = es[0] + es[1] + es[2] + es[3]
    member = jnp.zeros((tm, LANES), F32)
    for hot in hots:
        member = member + jnp.where(hot, 1.0, 0.0)
    row = lax.broadcasted_iota(I32, (tm, tm), 0)
    col = lax.broadcasted_iota(I32, (tm, tm), 1)
    tri = jnp.where(row > col, 1.0, 0.0).astype(BF16)
    before = _dot(tri, member.astype(BF16)) + carry[...]
    carry[...] = carry[...] + jnp.sum(member, axis=0, keepdims=True)
    ti = jnp.zeros((tm, LANES), I32)
    tg = jnp.zeros((tm, LANES), F32)
    rk = jnp.zeros((tm, LANES), I32)
    for k in range(TOP_K):
        rank = jnp.sum(jnp.where(hots[k], before, 0.0), axis=1, keepdims=True).astype(I32)
        ti = jnp.where(lane == k, idxs[k], ti)
        tg = jnp.where(lane == k, es[k] / den, tg)
        rk = jnp.where(lane == k, rank, rk)
    ti_ref[...] = ti
    tg_ref[...] = tg
    rk_ref[...] = rk
    cnt_ref[...] = jnp.broadcast_to(carry[...], cnt_ref.shape)


def moe_router(x2, g, scale, shift, rw, rb, seq, tm=256):
    n_tok, d = x2.shape
    tm = min(tm, seq)
    rwp = jnp.zeros((d, LANES), F32).at[:, :N_EXPERTS].set(rw)
    rbp = jnp.full((1, LANES), NEG, F32).at[0, :N_EXPERTS].set(rb)
    row = lambda i: (i, 0)
    bsel = lambda i: (i * tm // seq, 0, 0)
    return pl.pallas_call(
        _router_kernel,
        out_shape=[jax.ShapeDtypeStruct((n_tok, d // 2), I32),
                   jax.ShapeDtypeStruct((n_tok, LANES), I32),
                   jax.ShapeDtypeStruct((n_tok, LANES), F32),
                   jax.ShapeDtypeStruct((n_tok, LANES), I32),
                   jax.ShapeDtypeStruct((8, LANES), F32)],
        grid=(n_tok // tm,),
        in_specs=[pl.BlockSpec((tm, d), row),
                  pl.BlockSpec((1, d), lambda i: (0, 0)),
                  pl.BlockSpec((1, 1, d), bsel),
                  pl.BlockSpec((1, 1, d), bsel),
                  pl.BlockSpec((d, LANES), lambda i: (0, 0)),
                  pl.BlockSpec((1, LANES), lambda i: (0, 0))],
        out_specs=[pl.BlockSpec((tm, d // 2), row),
                   pl.BlockSpec((tm, LANES), row),
                   pl.BlockSpec((tm, LANES), row),
                   pl.BlockSpec((tm, LANES), row),
                   pl.BlockSpec((8, LANES), lambda i: (0, 0))],
        scratch_shapes=[pltpu.VMEM((1, LANES), F32)],
        compiler_params=_cparams(("arbitrary",)),
        name="moe_router",
    )(x2, g.reshape(1, d), scale.reshape(-1, 1, d), shift.reshape(-1, 1, d), rwp, rbp)


def _dispatch_kernel(dest_ref, h_ref, xs_in_ref, xs_ref, sem):
    del xs_in_ref
    tm = h_ref.shape[0]

    def row_copy(i, d):
        return pltpu.make_async_copy(h_ref.at[pl.ds(i, 1)], xs_ref.at[pl.ds(d, 1)], sem)

    def issue(i, c):
        for k in range(TOP_K):
            row_copy(i, dest_ref[i * TOP_K + k]).start()
        return c

    lax.fori_loop(0, tm, issue, 0)

    def drain(i, c):
        for k in range(TOP_K):
            row_copy(0, 0).wait()
        return c

    lax.fori_loop(0, tm, drain, 0)


def moe_dispatch(h32, dest_flat, rows, tm=256):
    n_tok, dw = h32.shape
    tm = min(tm, n_tok)
    xs0 = jnp.zeros((rows, dw), I32)
    return pl.pallas_call(
        _dispatch_kernel,
        out_shape=jax.ShapeDtypeStruct((rows, dw), I32),
        grid=(n_tok // tm,),
        in_specs=[pl.BlockSpec((tm * TOP_K,), lambda i: (i,), memory_space=pltpu.SMEM),
                  pl.BlockSpec((tm, dw), lambda i: (i, 0)),
                  pl.BlockSpec(memory_space=pl.ANY)],
        out_specs=pl.BlockSpec(memory_space=pl.ANY),
        scratch_shapes=[pltpu.SemaphoreType.DMA(())],
        input_output_aliases={2: 0},
        compiler_params=_cparams(("arbitrary",)),
        name="moe_dispatch",
    )(dest_flat, h32, xs0)


def _experts_kernel(ce_ref, nu_ref, x_ref, w1_ref, b1_ref, w2_ref, b2_ref, y_ref, act_scr, *, tn):
    c = pl.program_id(0)
    de = act_scr.shape[1]

    @pl.when(c < nu_ref[0])
    def _():
        words = x_ref[...]
        x = jnp.concatenate([pltpu.bitcast(words << 16, F32),
                             pltpu.bitcast(words & jnp.int32(-65536), F32)], axis=1).astype(BF16)
        for j in range(de // tn):
            sl = slice(j * tn, (j + 1) * tn)
            sl2 = slice(de + j * tn, de + (j + 1) * tn)
            glu = _dot(x, w1_ref[0, :, sl]) + b1_ref[0, :, sl]
            lin = _dot(x, w1_ref[0, :, sl2]) + b1_ref[0, :, sl2]
            glu = jnp.minimum(glu, SWIGLU_LIMIT)
            lin = jnp.clip(lin, -SWIGLU_LIMIT, SWIGLU_LIMIT)
            act_scr[:, sl] = (glu * jax.nn.sigmoid(SWIGLU_ALPHA * glu) * (lin + 1.0)).astype(BF16)
        a = act_scr[...]
        d = y_ref.shape[1]
        for j in range(d // tn):
            sl = slice(j * tn, (j + 1) * tn)
            y_ref[:, sl] = _dot(a, w2_ref[0, :, sl]) + b2_ref[0, :, sl]

    @pl.when(c >= nu_ref[0])
    def _():
        y_ref[...] = jnp.zeros_like(y_ref)


def moe_experts(xs, chunk_exp, n_used, w1, b1, w2, b2, tn=256):
    rows = xs.shape[0]
    ne, d, de2 = w1.shape
    de = de2 // 2
    n_chunks = rows // MOE_CHUNK
    grid_spec = pltpu.PrefetchScalarGridSpec(
        num_scalar_prefetch=2,
        grid=(n_chunks,),
        in_specs=[pl.BlockSpec((MOE_CHUNK, d // 2), lambda c, ce, nu: (c, 0)),
                  pl.BlockSpec((1, d, de2), lambda c, ce, nu: (ce[c], 0, 0)),
                  pl.BlockSpec((1, 1, de2), lambda c, ce, nu: (ce[c], 0, 0)),
                  pl.BlockSpec((1, de, d), lambda c, ce, nu: (ce[c], 0, 0)),
                  pl.BlockSpec((1, 1, d), lambda c, ce, nu: (ce[c], 0, 0))],
        out_specs=pl.BlockSpec((MOE_CHUNK, d), lambda c, ce, nu: (c, 0)),
        scratch_shapes=[pltpu.VMEM((MOE_CHUNK, de), BF16)],
    )
    return pl.pallas_call(
        functools.partial(_experts_kernel, tn=min(tn, de)),
        out_shape=jax.ShapeDtypeStruct((rows, d), F32),
        grid_spec=grid_spec,
        compiler_params=_cparams(("arbitrary",), VMEM_LIMIT_BYTES),
        name="moe_experts",
    )(chunk_exp, n_used, xs, w1, b1.reshape(ne, 1, de2), w2, b2.reshape(ne, 1, d))


def _combine_kernel(dest_ref, y_ref, tg_ref, x_ref, gate_ref, o_ref, buf, sem):
    tm = x_ref.shape[0]

    def row_copy(i, k, d):
        return pltpu.make_async_copy(y_ref.at[pl.ds(d, 1)], buf.at[k, pl.ds(i, 1)], sem)

    def issue(i, c):
        for k in range(TOP_K):
            row_copy(i, k, dest_ref[i * TOP_K + k]).start()
        return c

    lax.fori_loop(0, tm, issue, 0)

    def drain(i, c):
        for k in range(TOP_K):
            row_copy(0, 0, 0).wait()
        return c

    lax.fori_loop(0, tm, drain, 0)
    tg = tg_ref[...]
    moe = tg[:, 0:1] * buf[0]
    for k in range(1, TOP_K):
        moe = moe + tg[:, k:k + 1] * buf[k]
    o_ref[...] = x_ref[...] + gate_ref[0] * moe


def moe_combine(y, dest_flat, tg, x2, gate, seq, tm=256):
    n_tok, d = x2.shape
    tm = min(tm, seq)
    row = lambda i: (i, 0)
    return pl.pallas_call(
        _combine_kernel,
        out_shape=jax.ShapeDtypeStruct((n_tok, d), F32),
        grid=(n_tok // tm,),
        in_specs=[pl.BlockSpec((tm * TOP_K,), lambda i: (i,), memory_space=pltpu.SMEM),
                  pl.BlockSpec(memory_space=pl.ANY),
                  pl.BlockSpec((tm, LANES), row),
                  pl.BlockSpec((tm, d), row),
                  pl.BlockSpec((1, 1, d), lambda i: (i * tm // seq, 0, 0))],
        out_specs=pl.BlockSpec((tm, d), row),
        scratch_shapes=[pltpu.VMEM((TOP_K, tm, d), F32), pltpu.SemaphoreType.DMA(())],
        compiler_params=_cparams(("arbitrary",), VMEM_LIMIT_BYTES),
        name="moe_combine",
    )(dest_flat, y, tg, x2, gate.reshape(-1, 1, d))


def moe_block(x2, g, scale, shift, gate, rw, rb, w1, b1, w2, b2, seq):
    n_tok, d = x2.shape
    h, ti, tg, rk, cnt = moe_router(x2, g, scale, shift, rw, rb, seq)
    counts = cnt[0, :N_EXPERTS].astype(I32)
    padded = ((counts + MOE_CHUNK - 1) // MOE_CHUNK) * MOE_CHUNK
    ends_pad = jnp.cumsum(padded)
    start_pad = ends_pad - padded
    n_chunks = -(-(n_tok * TOP_K) // MOE_CHUNK) + N_EXPERTS
    rows = n_chunks * MOE_CHUNK
    experts = jnp.arange(N_EXPERTS, dtype=I32)
    slot_start = jnp.sum(jnp.where(ti[:, :TOP_K, None] == experts, start_pad, 0), axis=-1)
    dest = (slot_start + rk[:, :TOP_K]).reshape(-1)
    chunk_pos = jnp.arange(n_chunks, dtype=I32)[:, None] * MOE_CHUNK
    chunk_exp = jnp.minimum(jnp.sum((ends_pad[None, :] <= chunk_pos).astype(I32), axis=1), N_EXPERTS - 1)
    n_used = (ends_pad[-1:] // MOE_CHUNK).astype(I32)
    xs = moe_dispatch(h, dest, rows)
    y = moe_experts(xs, chunk_exp, n_used, w1.astype(BF16), b1, w2.astype(BF16), b2)
    return moe_combine(y, dest, tg, x2, gate, seq)


def _dsa_index_kernel(qi_ref, k2_ref, wi_ref, mask_ref, ka_scr, kb_scr, key_scr,
                      *, S, KC, n_sel, idx_scale):
    qb = pl.program_id(1)
    R = qi_ref.shape[1]
    nchunks = S // KC
    lane = lax.broadcasted_iota(I32, (1, LANES), 1)

    @pl.when(qb == 0)
    def _():
        kk = k2_ref[0]
        ka_scr[...] = jnp.where(lane < HEAD_DIM, kk, 0.0).astype(BF16)
        kb_scr[...] = jnp.where(lane >= HEAD_DIM, kk, 0.0).astype(BF16)

    nck = (qb * R + R + KC - 1) // KC
    w = wi_ref[0].astype(BF16).astype(F32)
    qv = qi_ref[0].astype(BF16)

    def relu_bf16(d):
        return jnp.maximum(d, 0.0).astype(BF16).astype(F32)

    row = qb * R + lax.broadcasted_iota(I32, (R, KC), 0)
    col0 = lax.broadcasted_iota(I32, (R, KC), 1)

    def score_chunk(c, carry):
        off = pl.multiple_of(c * KC, KC)
        ka = ka_scr[pl.ds(off, KC), :]
        kb = kb_scr[pl.ds(off, KC), :]
        sc = jnp.zeros((R, KC), F32)
        for pp in range(N_IDX_HEADS // 2):
            qp = qv[:, pp * LANES:(pp + 1) * LANES]
            sc = sc + w[:, 2 * pp:2 * pp + 1] * relu_bf16(_dot_nt(qp, ka))
            sc = sc + w[:, 2 * pp + 1:2 * pp + 2] * relu_bf16(_dot_nt(qp, kb))
        sc = sc * idx_scale
        sc = jnp.where(sc == 0.0, 0.0, sc)
        bits = pltpu.bitcast(sc, I32)
        key = jnp.where(bits < 0, bits ^ jnp.int32(0x7FFFFFFF), bits)
        key_scr[c] = jnp.where(off + col0 <= row, key, INT_MIN)
        return carry

    lax.fori_loop(0, nck, score_chunk, 0)

    def count(pred):
        def body(c, acc):
            kc = key_scr[c]
            for j in range(KC // LANES):
                acc = acc + jnp.where(pred(kc[:, j * LANES:(j + 1) * LANES], c * KC + j * LANES), 1.0, 0.0)
            return acc
        acc = lax.fori_loop(0, nck, body, jnp.zeros((R, LANES), F32))
        return jnp.broadcast_to(jnp.sum(acc, axis=1, keepdims=True), (R, LANES))

    nsel_f = float(n_sel)

    def bit_step(t, carry):
        cur, cnt_cur = carry
        cand = cur + (jnp.int32(1) << (31 - t))
        cnt = count(lambda kc, base: kc >= cand)
        ok = cnt >= nsel_f
        return jnp.where(ok, cand, cur), jnp.where(ok, cnt, cnt_cur)

    tau, cnt_tau = lax.fori_loop(
        0, 32, bit_step,
        (jnp.full((R, LANES), INT_MIN, I32), jnp.full((R, LANES), float(S), F32)))

    overflow = jnp.logical_and(tau > INT_MIN, cnt_tau > nsel_f)
    any_over = jnp.max(jnp.where(overflow, 1.0, 0.0)) > 0.0
    lane_r = lax.broadcasted_iota(I32, (R, LANES), 1)

    def write_mask(keep_fn):
        for c in range(nchunks):
            @pl.when(c < nck)
            def _():
                kc = key_scr[c]
                for j in range(KC // LANES):
                    keep = keep_fn(kc[:, j * LANES:(j + 1) * LANES], c * KC + j * LANES + lane_r)
                    mask_ref[0, :, c * KC + j * LANES:c * KC + (j + 1) * LANES] = (
                        jnp.where(keep, 1, 0).astype(jnp.int8))

            @pl.when(c >= nck)
            def _():
                mask_ref[0, :, c * KC:(c + 1) * KC] = jnp.zeros((R, KC), jnp.int8)

    @pl.when(jnp.logical_not(any_over))
    def _():
        floor = jnp.maximum(tau, INT_MIN + 1)
        write_mask(lambda kj, pos: kj >= floor)

    @pl.when(any_over)
    def _():
        need = nsel_f - count(lambda kc, base: kc > tau)

        def jstep(t, cur):
            cand = cur + (jnp.int32(1) << (S.bit_length() - 2 - t))
            cnt = count(lambda kc, base: jnp.logical_and(kc == tau, base + lane_r < cand))
            return jnp.where(cnt < need, cand, cur)

        jlim = lax.fori_loop(0, S.bit_length() - 1, jstep, jnp.zeros((R, LANES), I32))
        write_mask(lambda kj, pos: jnp.logical_and(
            jnp.logical_or(kj > tau, jnp.logical_and(kj == tau, pos <= jlim)), kj > INT_MIN))


def dsa_index_mask(aux, qcol, kcol, wcol, n_sel, R=128, KC=512):
    bsz, seq, _ = aux.shape
    R = min(R, seq)
    KC = min(KC, seq)
    nq = N_IDX_HEADS * HEAD_DIM
    return pl.pallas_call(
        functools.partial(_dsa_index_kernel, S=seq, KC=KC, n_sel=n_sel,
                          idx_scale=float(nq ** -0.5)),
        out_shape=jax.ShapeDtypeStruct((bsz, seq, seq), jnp.int8),
        grid=(bsz, seq // R),
        in_specs=[pl.BlockSpec((1, R, nq), lambda b, q: (b, q, qcol // nq)),
                  pl.BlockSpec((1, seq, LANES), lambda b, q: (b, 0, kcol // LANES)),
                  pl.BlockSpec((1, R, LANES), lambda b, q: (b, q, wcol // LANES))],
        out_specs=pl.BlockSpec((1, R, seq), lambda b, q: (b, q, 0)),
        scratch_shapes=[pltpu.VMEM((seq, LANES), BF16), pltpu.VMEM((seq, LANES), BF16),
                        pltpu.VMEM((seq // KC, R, KC), I32)],
        compiler_params=_cparams(("arbitrary", "arbitrary"), VMEM_LIMIT_BYTES),
        name="dsa_index_mask",
    )(aux, aux, aux)


def _dsa_attn_kernel(qi_ref, ki_ref, q_ref, k_ref, v_ref, mask_ref, tz_ref, o_ref,
                     m_scr, l_scr, acc_scr, *, TQ, TK, DCL):
    s = pl.program_id(1)
    qi = qi_ref[s]
    ki = ki_ref[s]
    k_last = ((qi + 1) * TQ - 1) // TK

    @pl.when(ki == 0)
    def _():
        m_scr[...] = jnp.full_like(m_scr, NEG)
        l_scr[...] = jnp.zeros_like(l_scr)
        acc_scr[...] = jnp.zeros_like(acc_scr)

    lane = lax.broadcasted_iota(I32, (1, LANES), 1)
    lo_half = lane < HEAD_DIM
    d0 = qi * TQ - ki * TK
    cb0 = (DCL - jnp.minimum(d0, DCL)) // LANES
    madd = jnp.where(mask_ref[0].astype(I32) != 0, 0.0, NEG)
    npair = q_ref.shape[2] // LANES
    for pp in range(npair):
        cs = slice(pp * LANES, (pp + 1) * LANES)
        q = q_ref[0, :, cs] * QK_SCALE
        k = k_ref[0, :, cs]
        v = v_ref[0, :, cs]
        pvs, alphas = [], []
        for hh in range(2):
            h = 2 * pp + hh
            qsel = lo_half if hh == 0 else jnp.logical_not(lo_half)
            bias = jnp.concatenate([tz_ref[h, cb0 + jj] for jj in range(TK // LANES)], axis=1)
            sc = _dot_nt(_keep_lanes(q, qsel), k) + bias + madd
            m_scr[h], l_scr[h], alpha, pm = _online_softmax(sc, m_scr[h], l_scr[h])
            pvs.append(_dot(pm, v))
            alphas.append(alpha)
        acc_scr[:, cs] = (acc_scr[:, cs] * jnp.where(lo_half, alphas[0], alphas[1])
                          + jnp.where(lo_half, pvs[0], pvs[1]))

    @pl.when(ki == k_last)
    def _():
        for pp in range(npair):
            cs = slice(pp * LANES, (pp + 1) * LANES)
            l = jnp.where(lo_half, l_scr[2 * pp], l_scr[2 * pp + 1])
            o_ref[0, :, cs] = (acc_scr[:, cs] / l).astype(o_ref.dtype)


def _dsa_bias_table(rel_bias, seq, TQ, TK, DCL):
    i = np.arange(TQ)[:, None]
    c = np.arange(DCL + TK)[None, :]
    d = np.clip(i - c + DCL, 0, seq - 1)
    tz = _bucket_bias(jnp.asarray(d, I32), rel_bias)
    tz = tz.reshape(rel_bias.shape[1], TQ, (DCL + TK) // LANES, LANES)
    return tz.transpose(0, 2, 1, 3)


def _np_t5_bucket(d):
    max_exact = N_BUCKETS // 2
    df = np.maximum(d, 1).astype(np.float32)
    large = max_exact + (np.log(df / max_exact) / math.log(MAX_DISTANCE / max_exact)
                         * (N_BUCKETS - max_exact)).astype(np.int32)
    return np.where(d < max_exact, d, np.minimum(large, N_BUCKETS - 1))


def dsa_attention(q_src, qcol, kv, mask, rel_bias, TQ=128, TK=1024):
    bsz, seq, _ = q_src.shape
    TQ = min(TQ, seq)
    TK = min(TK, seq)
    hw = N_DSA_HEADS * HEAD_DIM
    buckets = _np_t5_bucket(np.arange(seq))
    first_const = int(np.max(np.nonzero(buckets != buckets[-1])[0])) + 1
    DCL = min(seq, -(-(first_const + TK - 1) // LANES) * LANES)
    tz = _dsa_bias_table(rel_bias, seq, TQ, TK, DCL)
    qs, ks = _tri_schedule(seq // TQ, TK // TQ)
    grid_spec = pltpu.PrefetchScalarGridSpec(
        num_scalar_prefetch=2,
        grid=(bsz, qs.shape[0]),
        in_specs=[
            pl.BlockSpec((1, TQ, hw), lambda b, s, qi, ki: (b, qi[s], qcol // hw)),
            pl.BlockSpec((1, TK, hw), lambda b, s, qi, ki: (b, ki[s], 0)),
            pl.BlockSpec((1, TK, hw), lambda b, s, qi, ki: (b, ki[s], 1)),
            pl.BlockSpec((1, TQ, TK), lambda b, s, qi, ki: (b, qi[s], ki[s])),
            pl.BlockSpec(tz.shape, lambda b, s, qi, ki: (0, 0, 0, 0), pipeline_mode=pl.Buffered(1)),
        ],
        out_specs=pl.BlockSpec((1, TQ, hw), lambda b, s, qi, ki: (b, qi[s], 0)),
        scratch_shapes=[pltpu.VMEM((N_DSA_HEADS, TQ, LANES), F32),
                        pltpu.VMEM((N_DSA_HEADS, TQ, LANES), F32),
                        pltpu.VMEM((TQ, hw), F32)],
    )
    return pl.pallas_call(
        functools.partial(_dsa_attn_kernel, TQ=TQ, TK=TK, DCL=DCL),
        out_shape=jax.ShapeDtypeStruct((bsz, seq, hw), BF16),
        grid_spec=grid_spec,
        compiler_params=_cparams(("arbitrary", "arbitrary"), VMEM_LIMIT_BYTES),
        name="dsa_attention",
    )(qs, ks, q_src, kv, kv, mask, tz)


def _rms_kernel(x_ref, g_ref, o_ref):
    x = x_ref[...]
    ms = jnp.mean(x * x, axis=-1, keepdims=True)
    o_ref[...] = x * lax.rsqrt(ms + EPS) * g_ref[...]


def final_rmsnorm(x2, g, tm=512):
    n_tok, d = x2.shape
    tm = min(tm, n_tok)
    return pl.pallas_call(
        _rms_kernel,
        out_shape=jax.ShapeDtypeStruct((n_tok, d), F32),
        grid=(n_tok // tm,),
        in_specs=[pl.BlockSpec((tm, d), lambda i: (i, 0)), pl.BlockSpec((1, d), lambda i: (0, 0))],
        out_specs=pl.BlockSpec((tm, d), lambda i: (i, 0)),
        compiler_params=_cparams(("arbitrary",)),
        name="final_rmsnorm",
    )(x2, g.reshape(1, d))


def even_mixer_residual(x2, shift1, scale1, gate1, norm1, w_in, fox_fb, w_out, rel_bias, bsz, seq):
    d = x2.shape[1]
    nfq = 3 * N_FOX_HEADS * HEAD_DIM
    w_main = jnp.concatenate([w_in[:, :nfq], w_in[:, nfq + N_FOX_HEADS:]], axis=1).astype(BF16)
    rep = np.repeat(np.arange(N_FOX_HEADS), FOX_PARTS)
    gcols = np.concatenate([rep, rep])
    w_gate = jnp.zeros((d, LANES), F32).at[:, :2 * FOX_XW].set(w_in[:, nfq + gcols])
    fb = jnp.zeros((1, LANES), F32).at[0, :2 * FOX_XW].set(fox_fb[gcols])
    proj, gate_z = norm_mod_matmul(x2, norm1, scale1, shift1, w_main, w_gate, seq)
    c = proj.shape[1]
    proj3 = proj.reshape(bsz, seq, c)
    qx, kx = fox_gate_columns(gate_z.reshape(bsz, seq, LANES), fb)
    fox = fox_attention(proj3, qx, kx, N_FOX_HEADS)
    dil_o, dil_l = [], []
    for g, (window, dil) in enumerate(DIL_PAIRS):
        bias = _dil_bias_tiles(rel_bias, g, window, dil)
        o, l = dilated_group_attention(proj3, bias, g, dil, nfq + g * N_DIL_SLOTS * HEAD_DIM)
        dil_o.append(o)
        dil_l.append(l)
    return out_proj_even(fox.reshape(bsz * seq, -1), dil_o, dil_l, w_out.astype(BF16), x2, gate1, seq)


def odd_mixer_residual(x2, shift1, scale1, gate1, norm1, w_in, kv_norm, w_ukv, w_out, rel_bias, bsz, seq):
    d = x2.shape[1]
    nq = N_DSA_HEADS * HEAD_DIM
    ni = N_IDX_HEADS * HEAD_DIM
    c_ckv, c_qi = nq, nq + KV_RANK
    c_ki = c_qi + ni
    c_wi = c_ki + HEAD_DIM
    w_main = w_in[:, :nq].astype(BF16)
    c_w, c_k, c_q = KV_RANK, KV_RANK + LANES, KV_RANK + 2 * LANES
    w_aux = jnp.zeros((d, c_q + ni), F32)
    w_aux = w_aux.at[:, :KV_RANK].set(w_in[:, c_ckv:c_qi])
    w_aux = w_aux.at[:, c_w:c_w + N_IDX_HEADS].set(w_in[:, c_wi:])
    w_aux = w_aux.at[:, c_k:c_k + HEAD_DIM].set(w_in[:, c_ki:c_wi])
    w_aux = w_aux.at[:, c_k + HEAD_DIM:c_q].set(w_in[:, c_ki:c_wi])
    w_aux = w_aux.at[:, c_q:].set(w_in[:, c_qi:c_ki])
    proj, aux = norm_mod_matmul(x2, norm1, scale1, shift1, w_main, w_aux, seq)
    zeros = jnp.zeros((bsz, KV_RANK), F32)
    kv = norm_mod_matmul(aux[:, :KV_RANK], kv_norm, zeros, zeros, w_ukv.astype(BF16), None, seq)
    proj3 = proj.reshape(bsz, seq, -1)
    n_sel = min(DSA_TOPK, seq // 4)
    mask = dsa_index_mask(aux.reshape(bsz, seq, -1), c_q, c_k, c_w, n_sel)
    att = dsa_attention(proj3, 0, kv.reshape(bsz, seq, -1), mask, rel_bias)
    return out_proj_odd(att.reshape(bsz * seq, -1), w_out.astype(BF16), x2, gate1, seq)


def kernel(x, c, rel_bias, l0_norm1, l0_ada_w, l0_ada_b, l0_w_in, l0_fox_fb, l0_w_out, l0_norm2, l0_router_w, l0_router_b, l0_w1, l0_b1, l0_w2, l0_b2, l1_norm1, l1_ada_w, l1_ada_b, l1_w_in, l1_kv_norm, l1_w_ukv, l1_w_out, l1_norm2, l1_router_w, l1_router_b, l1_w1, l1_b1, l1_w2, l1_b2, final_norm):
    bsz, seq, d = x.shape
    x2 = x.reshape(bsz * seq, d)
    layers = (
        (l0_norm1, l0_ada_w, l0_ada_b, l0_norm2, l0_router_w, l0_router_b, l0_w1, l0_b1, l0_w2, l0_b2),
        (l1_norm1, l1_ada_w, l1_ada_b, l1_norm2, l1_router_w, l1_router_b, l1_w1, l1_b1, l1_w2, l1_b2),
    )
    for i, (norm1, ada_w, ada_b, norm2, rw, rb, w1, b1, w2, b2) in enumerate(layers):
        mods = ada_mods(c, ada_w, ada_b)
        shift1, scale1, gate1, shift2, scale2, gate2 = (mods[:, j * d:(j + 1) * d] for j in range(6))
        if i % 2 == 0:
            x2 = even_mixer_residual(x2, shift1, scale1, gate1, norm1, l0_w_in, l0_fox_fb, l0_w_out,
                                     rel_bias, bsz, seq)
        else:
            x2 = odd_mixer_residual(x2, shift1, scale1, gate1, norm1, l1_w_in, l1_kv_norm, l1_w_ukv,
                                    l1_w_out, rel_bias, bsz, seq)
        x2 = moe_block(x2, norm2, scale2, shift2, gate2, rw, rb, w1, b1, w2, b2, seq)
    return final_rmsnorm(x2, final_norm).reshape(bsz, seq, d)
```

```python
import functools
import math

import numpy as np
import jax
import jax.numpy as jnp
from jax import lax
from jax.experimental import pallas as pl
from jax.experimental.pallas import tpu as pltpu

F32, BF16, I32 = jnp.float32, jnp.bfloat16, jnp.int32

LANES = 128
VMEM_LIMIT_BYTES = 56 * 1024 * 1024

HEAD_DIM = 64
N_FOX_HEADS = 8
DIL_PAIRS = ((128, 1), (512, 4), (2048, 16))
N_DIL_SLOTS = 4
N_DSA_HEADS = 16
KV_RANK = 256
N_IDX_HEADS = 8
DSA_TOPK = 256
N_BUCKETS = 32
MAX_DISTANCE = 2048
N_EXPERTS = 32
TOP_K = 4
D_EXPERT = 1024
SWIGLU_ALPHA = 1.702
SWIGLU_LIMIT = 7.0
MOE_CHUNK = 256
EPS = 1e-6

NEG = -1e30
INT_MIN = -(2 ** 31)
QK_SCALE = HEAD_DIM ** -0.5
FOX_PARTS = 3
FOX_XW = FOX_PARTS * N_FOX_HEADS


def _cparams(sem, vmem=None):
    return pltpu.CompilerParams(dimension_semantics=sem, vmem_limit_bytes=vmem)


def _dot_nt(a, b):
    return lax.dot_general(a, b, (((1,), (1,)), ((), ())), preferred_element_type=F32)


def _dot(a, b):
    return jnp.dot(a, b, preferred_element_type=F32)


def _dot_f32(a, b):
    return jnp.dot(a, b, precision=lax.Precision.HIGHEST, preferred_element_type=F32)


def _keep_lanes(x, sel):
    return jnp.where(sel, x.astype(F32), 0.0).astype(x.dtype)


def _online_softmax(sc, m_prev):
    cols = [sc[:, j:j + LANES] for j in range(0, sc.shape[1], LANES)]
    m_cur = functools.reduce(jnp.maximum, cols).astype(F32)
    m_new = jnp.maximum(m_prev, jnp.max(m_cur, axis=1, keepdims=True))
    alpha = jnp.exp(m_prev - m_new)
    shift = m_new.astype(BF16)
    return m_new, alpha, jnp.concatenate([jnp.exp(c - shift) for c in cols], axis=1)


def _values_with_ones(v, n_heads):
    lead = v.shape[:-1]
    v4 = v.reshape(*lead, n_heads // 2, 2, HEAD_DIM)
    ones = jnp.ones((*lead, n_heads // 2, HEAD_DIM), v.dtype)
    va = jnp.concatenate([v4[..., 0, :], ones], axis=-1)
    vb = jnp.concatenate([ones, v4[..., 1, :]], axis=-1)
    return jnp.stack([va, vb], axis=-2).reshape(*lead, n_heads * 2 * HEAD_DIM)


def _tri_schedule(nq, kq_ratio=1):
    qs, ks = [], []
    for qi in range(nq):
        for ki in range(qi // kq_ratio + 1):
            qs.append(qi)
            ks.append(ki)
    return jnp.asarray(qs, I32), jnp.asarray(ks, I32)


def _ada_kernel(c_ref, w_ref, b_ref, o_ref):
    c = c_ref[...]
    sc = c * jax.nn.sigmoid(c)
    o_ref[...] = _dot_f32(sc, w_ref[...]) + b_ref[...]


def ada_mods(c, w, b):
    bsz, d = c.shape
    n = w.shape[1]
    tn = min(n, 1024)
    cp = jnp.zeros((8, d), F32).at[:bsz].set(c)
    out = pl.pallas_call(
        _ada_kernel,
        out_shape=jax.ShapeDtypeStruct((8, n), F32),
        grid=(n // tn,),
        in_specs=[pl.BlockSpec((8, d), lambda j: (0, 0)),
                  pl.BlockSpec((d, tn), lambda j: (0, j)),
                  pl.BlockSpec((1, tn), lambda j: (0, j))],
        out_specs=pl.BlockSpec((8, tn), lambda j: (0, j)),
        compiler_params=_cparams(("arbitrary",)),
        name="ada_mods",
    )(cp, w, b.reshape(1, n))
    return out[:bsz]


def _nmm_kernel(x_ref, g_ref, sc_ref, sh_ref, w_ref, *rest, tn, has_aux):
    if has_aux:
        wa_ref, o_ref, oa_ref = rest
    else:
        (o_ref,) = rest
    x = x_ref[...]
    ms = jnp.mean(x * x, axis=-1, keepdims=True)
    y = x * lax.rsqrt(ms + EPS) * g_ref[...]
    hf = y * (1.0 + sc_ref[0]) + sh_ref[0]
    h = hf.astype(BF16)
    n = o_ref.shape[1]
    for j0 in range(0, n, tn):
        j1 = min(j0 + tn, n)
        o_ref[:, j0:j1] = _dot(h, w_ref[:, j0:j1]).astype(o_ref.dtype)
    if has_aux:
        na = oa_ref.shape[1]
        for j0 in range(0, na, tn):
            j1 = min(j0 + tn, na)
            oa_ref[:, j0:j1] = _dot_f32(hf, wa_ref[:, j0:j1])


def norm_mod_matmul(x2, g, scale, shift, w, w_aux, seq, tm=512, tn=256):
    n_tok, din = x2.shape
    nout = w.shape[1]
    tm = min(tm, seq)
    has_aux = w_aux is not None
    bsel = lambda i: (i * tm // seq, 0, 0)
    in_specs = [pl.BlockSpec((tm, din), lambda i: (i, 0)),
                pl.BlockSpec((1, din), lambda i: (0, 0)),
                pl.BlockSpec((1, 1, din), bsel),
                pl.BlockSpec((1, 1, din), bsel),
                pl.BlockSpec((din, nout), lambda i: (0, 0))]
    out_shape = [jax.ShapeDtypeStruct((n_tok, nout), BF16)]
    out_specs = [pl.BlockSpec((tm, nout), lambda i: (i, 0))]
    args = [x2, g.reshape(1, din), scale.reshape(-1, 1, din), shift.reshape(-1, 1, din), w]
    if has_aux:
        na = w_aux.shape[1]
        in_specs.append(pl.BlockSpec((din, na), lambda i: (0, 0)))
        out_shape.append(jax.ShapeDtypeStruct((n_tok, na), F32))
        out_specs.append(pl.BlockSpec((tm, na), lambda i: (i, 0)))
        args.append(w_aux)
    outs = pl.pallas_call(
        functools.partial(_nmm_kernel, tn=min(tn, nout), has_aux=has_aux),
        out_shape=out_shape,
        grid=(n_tok // tm,),
        in_specs=in_specs,
        out_specs=out_specs,
        compiler_params=_cparams(("arbitrary",), VMEM_LIMIT_BYTES),
        name="norm_mod_matmul",
    )(*args)
    return outs if has_aux else outs[0]


def _foxcum_kernel(z_ref, fb_ref, qx_ref, kx_ref, carry):
    t = pl.program_id(1)

    @pl.when(t == 0)
    def _():
        carry[...] = jnp.zeros_like(carry)

    ts = z_ref.shape[1]
    z = z_ref[0] + fb_ref[...]
    lf = jnp.minimum(z, 0.0) - jnp.log1p(jnp.exp(-jnp.abs(z)))
    row = lax.broadcasted_iota(I32, (ts, ts), 0)
    col = lax.broadcasted_iota(I32, (ts, ts), 1)
    tri = jnp.where(row >= col, 1.0, 0.0).astype(F32)
    c = jnp.dot(tri, lf, precision=lax.Precision.HIGHEST, preferred_element_type=F32) + carry[...]
    carry[...] = c[ts - 1:ts, :]
    hi = c.astype(BF16).astype(F32)
    r1 = c - hi
    mid = r1.astype(BF16).astype(F32)
    lo = (r1 - mid).astype(BF16).astype(F32)
    lane = lax.broadcasted_iota(I32, (ts, LANES), 1)
    ph = lane % FOX_PARTS
    part = jnp.where(ph == 0, hi, jnp.where(ph == 1, mid, lo))
    first = lane < FOX_XW
    second = (lane >= FOX_XW) & (lane < 2 * FOX_XW)
    qx_ref[0] = jnp.where(first, part, jnp.where(second, 1.0, 0.0)).astype(BF16)
    kx_ref[0] = jnp.where(first, 1.0, jnp.where(second, -part, 0.0)).astype(BF16)


def fox_gate_columns(z, fb, ts=512):
    bsz, seq, _ = z.shape
    ts = min(ts, seq)
    return pl.pallas_call(
        _foxcum_kernel,
        out_shape=[jax.ShapeDtypeStruct((bsz, seq, LANES), BF16)] * 2,
        grid=(bsz, seq // ts),
        in_specs=[pl.BlockSpec((1, ts, LANES), lambda b, t: (b, t, 0)),
                  pl.BlockSpec((1, LANES), lambda b, t: (0, 0))],
        out_specs=[pl.BlockSpec((1, ts, LANES), lambda b, t: (b, t, 0))] * 2,
        scratch_shapes=[pltpu.VMEM((1, LANES), F32)],
        compiler_params=_cparams(("arbitrary", "arbitrary")),
        name="fox_gate_columns",
    )(z, fb)


def _fox_kernel(qi_ref, ki_ref, q_ref, k_ref, v_ref, qx_ref, kx_ref, o_ref, m_scr, l_scr, acc_scr, *, T):
    p = pl.program_id(1)
    s = pl.program_id(2)
    qi = qi_ref[s]
    ki = ki_ref[s]

    @pl.when(ki == 0)
    def _():
        m_scr[...] = jnp.full_like(m_scr, NEG)
        l_scr[...] = jnp.zeros_like(l_scr)
        acc_scr[...] = jnp.zeros_like(acc_scr)

    lane = lax.broadcasted_iota(I32, (1, LANES), 1)
    lo_half = lane < HEAD_DIM

    def step(masked):
        q = q_ref[0] * QK_SCALE
        qx = qx_ref[0]
        kcat = jnp.concatenate([k_ref[0], kx_ref[0]], axis=1)
        if masked:
            row = lax.broadcasted_iota(I32, (T, T), 0)
            col = lax.broadcasted_iota(I32, (T, T), 1)
            causal_add = jnp.where(col <= row, 0.0, NEG).astype(BF16)
        scs = []
        for hh in range(2):
            xl = FOX_PARTS * (2 * p + hh)
            qsel = lo_half if hh == 0 else jnp.logical_not(lo_half)
            xsel = ((lane >= xl) & (lane < xl + FOX_PARTS)) | (
                (lane >= FOX_XW + xl) & (lane < FOX_XW + xl + FOX_PARTS))
            qcat = jnp.concatenate([_keep_lanes(q, qsel), _keep_lanes(qx, xsel)], axis=1)
            sc = _dot_nt(qcat, kcat).astype(BF16)
            scs.append(sc + causal_add if masked else sc)
        pvs, alphas = [], []
        for hh in range(2):
            m_scr[hh], alpha, pm = _online_softmax(scs[hh], m_scr[hh])
            pvs.append(_dot(pm, v_ref[0, :, hh * LANES:(hh + 1) * LANES]))
            alphas.append(alpha)
        acc_scr[...] = (acc_scr[...] * jnp.where(lo_half, alphas[0], alphas[1])
                        + jnp.where(lo_half, pvs[0], pvs[1]))
        l_scr[...] = (l_scr[...] * jnp.where(lo_half, alphas[1], alphas[0])
                      + jnp.where(lo_half, pvs[1], pvs[0]))

    @pl.when(ki < qi)
    def _():
        step(False)

    @pl.when(ki == qi)
    def _():
        step(True)
        l = pltpu.roll(l_scr[...], HEAD_DIM, axis=1)
        o_ref[0] = (acc_scr[...] / l).astype(o_ref.dtype)


def fox_attention(proj, qx, kx, n_heads, T=512):
    bsz, seq, _ = proj.shape
    T = min(T, seq)
    npair = n_heads // 2
    hw = n_heads * HEAD_DIM
    v_ones = _values_with_ones(proj[:, :, 2 * hw:3 * hw], n_heads)
    qs, ks = _tri_schedule(seq // T)
    grid_spec = pltpu.PrefetchScalarGridSpec(
        num_scalar_prefetch=2,
        grid=(bsz, npair, qs.shape[0]),
        in_specs=[
            pl.BlockSpec((1, T, LANES), lambda b, p, s, qi, ki: (b, qi[s], p)),
            pl.BlockSpec((1, T, LANES), lambda b, p, s, qi, ki: (b, ki[s], npair + p)),
            pl.BlockSpec((1, T, 2 * LANES), lambda b, p, s, qi, ki: (b, ki[s], p)),
            pl.BlockSpec((1, T, LANES), lambda b, p, s, qi, ki: (b, qi[s], 0)),
            pl.BlockSpec((1, T, LANES), lambda b, p, s, qi, ki: (b, ki[s], 0)),
        ],
        out_specs=pl.BlockSpec((1, T, LANES), lambda b, p, s, qi, ki: (b, qi[s], p)),
        scratch_shapes=[pltpu.VMEM((2, T, LANES), F32), pltpu.VMEM((T, LANES), F32),
                        pltpu.VMEM((T, LANES), F32)],
    )
    return pl.pallas_call(
        functools.partial(_fox_kernel, T=T),
        out_shape=jax.ShapeDtypeStruct((bsz, seq, npair * LANES), BF16),
        grid_spec=grid_spec,
        compiler_params=_cparams(("arbitrary", "arbitrary", "arbitrary"), VMEM_LIMIT_BYTES),
        name="fox_attention",
    )(qs, ks, proj, proj, v_ones, qx, kx)


def _dil_kernel(q_ref, kp_ref, kc_ref, vp_ref, vc_ref, bias_ref, o_ref, lse_ref, *, W):
    ut = pl.program_id(3)
    lane = lax.broadcasted_iota(I32, (1, LANES), 1)
    lo_half = lane < HEAD_DIM
    q = q_ref[0] * QK_SCALE
    k = jnp.concatenate([kp_ref[0], kc_ref[0]], axis=0)
    v = jnp.concatenate([vp_ref[0], vc_ref[0]], axis=0)
    col = lax.broadcasted_iota(I32, (W, 2 * W), 1)
    no_prev = jnp.logical_and(ut == 0, col < W)
    outs, lses = [], []
    for hh in range(2):
        qsel = lo_half if hh == 0 else jnp.logical_not(lo_half)
        sc = _dot_nt(_keep_lanes(q, qsel), k) + bias_ref[0, hh]
        sc = jnp.where(no_prev, NEG, sc)
        mx = jnp.max(sc, axis=1, keepdims=True)
        e = jnp.exp(sc - mx)
        den = jnp.sum(e, axis=1, keepdims=True)
        outs.append(_dot((e / den).astype(BF16), v))
        lses.append(mx + jnp.log(den))
    o_ref[0] = jnp.where(lo_half, outs[0], outs[1])
    lse_ref[0] = jnp.where(lo_half, lses[0], lses[1])


def dilated_group_attention(proj, bias, g, dil, col0, W=128):
    bsz, seq, c = proj.shape
    sub = seq // dil
    pv = proj.reshape(bsz, sub, dil * c)
    cb = c // LANES
    qb = col0 // LANES
    kb = qb + 3 * N_DIL_SLOTS * HEAD_DIM // LANES
    vb = kb + 3 * N_DIL_SLOTS * HEAD_DIM // LANES
    prev = lambda u: jnp.maximum(u - 1, 0)
    out = pl.pallas_call(
        functools.partial(_dil_kernel, W=W),
        out_shape=[jax.ShapeDtypeStruct((bsz, sub, dil * 2 * LANES), F32)] * 2,
        grid=(bsz, dil, 2, sub // W),
        in_specs=[
            pl.BlockSpec((1, W, LANES), lambda b, a, p, u: (b, u, a * cb + qb + p)),
            pl.BlockSpec((1, W, LANES), lambda b, a, p, u: (b, prev(u), a * cb + kb + p)),
            pl.BlockSpec((1, W, LANES), lambda b, a, p, u: (b, u, a * cb + kb + p)),
            pl.BlockSpec((1, W, LANES), lambda b, a, p, u: (b, prev(u), a * cb + vb + p)),
            pl.BlockSpec((1, W, LANES), lambda b, a, p, u: (b, u, a * cb + vb + p)),
            pl.BlockSpec((1, 2, W, 2 * W), lambda b, a, p, u: (p, 0, 0, 0)),
        ],
        out_specs=[pl.BlockSpec((1, W, LANES), lambda b, a, p, u: (b, u, a * 2 + p))] * 2,
        compiler_params=_cparams(("arbitrary",) * 4),
        name=f"dilated_attention_g{g}",
    )(pv, pv, pv, pv, pv, bias)
    return [o.reshape(bsz * seq, 2 * LANES) for o in out]


def _t5_bucket(dist):
    max_exact = N_BUCKETS // 2
    d = jnp.maximum(dist, 0)
    df = jnp.maximum(d, 1).astype(F32)
    large = max_exact + (jnp.log(df / max_exact) / math.log(MAX_DISTANCE / max_exact)
                         * (N_BUCKETS - max_exact)).astype(I32)
    large = jnp.minimum(large, N_BUCKETS - 1)
    return jnp.where(d < max_exact, d, large)


def _bucket_bias(dist, table):
    onehot = (_t5_bucket(dist)[..., None] == jnp.arange(N_BUCKETS, dtype=I32)).astype(F32)
    out = jnp.einsum('...b,bh->h...', onehot, table.astype(F32), precision=lax.Precision.HIGHEST)
    return out


def _dil_bias_tiles(rel_bias, g, window, dil, W=128):
    assert window // dil == W
    i = np.arange(W)[:, None]
    j = np.arange(2 * W)[None, :]
    n = i + W - j
    valid = (n >= 0) & (n <= W)
    heads = rel_bias[:, g * N_DIL_SLOTS:(g + 1) * N_DIL_SLOTS]
    tiles = jnp.where(valid[None], _bucket_bias(jnp.asarray(np.clip(n, 0, W) * dil, I32), heads), NEG)
    return tiles.reshape(2, 2, W, 2 * W)


def _oproj0_kernel(fox_ref, o0_ref, o1_ref, o2_ref, l0_ref, l1_ref, l2_ref, w_ref, x_ref, gate_ref, out_ref):
    ls = [l0_ref[...], l1_ref[...], l2_ref[...]]
    os_ = [o0_ref[...], o1_ref[...], o2_ref[...]]
    mx = jnp.maximum(jnp.maximum(ls[0], ls[1]), ls[2])
    ws = [jnp.exp(l - mx) for l in ls]
    den = ws[0] + ws[1] + ws[2]
    dil = (ws[0] * os_[0] + ws[1] * os_[1] + ws[2] * os_[2]) / den
    nf = fox_ref.shape[1]
    mix = _dot(fox_ref[...], w_ref[:nf, :]) + _dot(dil.astype(BF16), w_ref[nf:, :])
    out_ref[...] = x_ref[...] + gate_ref[0] * mix


def out_proj_even(fox, dil_o, dil_l, w, x2, gate, seq, tm=512):
    n_tok, d = x2.shape
    tm = min(tm, seq)
    nf = fox.shape[1]
    nd = N_DIL_SLOTS * HEAD_DIM
    row = lambda i: (i, 0)
    return pl.pallas_call(
        _oproj0_kernel,
        out_shape=jax.ShapeDtypeStruct((n_tok, d), F32),
        grid=(n_tok // tm,),
        in_specs=[pl.BlockSpec((tm, nf), row)]
                 + [pl.BlockSpec((tm, nd), row)] * 6
                 + [pl.BlockSpec((nf + nd, d), lambda i: (0, 0)),
                    pl.BlockSpec((tm, d), row),
                    pl.BlockSpec((1, 1, d), lambda i: (i * tm // seq, 0, 0))],
        out_specs=pl.BlockSpec((tm, d), row),
        compiler_params=_cparams(("arbitrary",), VMEM_LIMIT_BYTES),
        name="out_proj_even",
    )(fox, *dil_o, *dil_l, w, x2, gate.reshape(-1, 1, d))


def _oproj_kernel(a_ref, w_ref, x_ref, gate_ref, out_ref):
    out_ref[...] = x_ref[...] + gate_ref[0] * _dot(a_ref[...], w_ref[...])


def out_proj_odd(a, w, x2, gate, seq, tm=512):
    n_tok, d = x2.shape
    tm = min(tm, seq)
    ka = a.shape[1]
    row = lambda i: (i, 0)
    return pl.pallas_call(
        _oproj_kernel,
        out_shape=jax.ShapeDtypeStruct((n_tok, d), F32),
        grid=(n_tok // tm,),
        in_specs=[pl.BlockSpec((tm, ka), row),
                  pl.BlockSpec((ka, d), lambda i: (0, 0)),
                  pl.BlockSpec((tm, d), row),
                  pl.BlockSpec((1, 1, d), lambda i: (i * tm // seq, 0, 0))],
        out_specs=pl.BlockSpec((tm, d), row),
        compiler_params=_cparams(("arbitrary",), VMEM_LIMIT_BYTES),
        name="out_proj_odd",
    )(a, w, x2, gate.reshape(-1, 1, d))


def _router_kernel(x_ref, g_ref, sc_ref, sh_ref, rw_ref, rb_ref,
                   h_ref, ti_ref, tg_ref, rk_ref, cnt_ref, carry):
    i = pl.program_id(0)

    @pl.when(i == 0)
    def _():
        carry[...] = jnp.zeros_like(carry)

    tm = x_ref.shape[0]
    x = x_ref[...]
    ms = jnp.mean(x * x, axis=-1, keepdims=True)
    y = x * lax.rsqrt(ms + EPS) * g_ref[...]
    hr = (y * (1.0 + sc_ref[0]) + sh_ref[0]).astype(BF16).astype(F32)
    bits = pltpu.bitcast(hr, I32)
    half = hr.shape[1] // 2
    h_ref[...] = lax.shift_right_logical(bits[:, :half], 16) | (bits[:, half:] & jnp.int32(-65536))
    logits = _dot_f32(hr, rw_ref[...]) + rb_ref[...]
    lane = lax.broadcasted_iota(I32, (tm, LANES), 1)
    l = logits
    vals, idxs, hots = [], [], []
    for _ in range(TOP_K):
        mx = jnp.max(l, axis=1, keepdims=True)
        idx = jnp.min(jnp.where(l == mx, lane, LANES), axis=1, keepdims=True)
        hot = lane == idx
        l = jnp.where(hot, -jnp.inf, l)
        vals.append(mx)
        idxs.append(idx)
        hots.append(hot)
    es = [jnp.exp(v - vals[0]) for v in vals]
    den = es[0] + es[1] + es[2] + es[3]
    member = jnp.zeros((tm, LANES), F32)
    for hot in hots:
        member = member + jnp.where(hot, 1.0, 0.0)
    row = lax.broadcasted_iota(I32, (tm, tm), 0)
    col = lax.broadcasted_iota(I32, (tm, tm), 1)
    tri = jnp.where(row > col, 1.0, 0.0).astype(BF16)
    before = _dot(tri, member.astype(BF16)) + carry[...]
    carry[...] = carry[...] + jnp.sum(member, axis=0, keepdims=True)
    ti = jnp.zeros((tm, LANES), I32)
    tg = jnp.zeros((tm, LANES), F32)
    rk = jnp.zeros((tm, LANES), I32)
    for k in range(TOP_K):
        rank = jnp.sum(jnp.where(hots[k], before, 0.0), axis=1, keepdims=True).astype(I32)
        ti = jnp.where(lane == k, idxs[k], ti)
        tg = jnp.where(lane == k, es[k] / den, tg)
        rk = jnp.where(lane == k, rank, rk)
    ti_ref[...] = ti
    tg_ref[...] = tg
    rk_ref[...] = rk
    cnt_ref[...] = jnp.broadcast_to(carry[...], cnt_ref.shape)


def moe_router(x2, g, scale, shift, rw, rb, seq, tm=256):
    n_tok, d = x2.shape
    tm = min(tm, seq)
    rwp = jnp.zeros((d, LANES), F32).at[:, :N_EXPERTS].set(rw)
    rbp = jnp.full((1, LANES), NEG, F32).at[0, :N_EXPERTS].set(rb)
    row = lambda i: (i, 0)
    bsel = lambda i: (i * tm // seq, 0, 0)
    return pl.pallas_call(
        _router_kernel,
        out_shape=[jax.ShapeDtypeStruct((n_tok, d // 2), I32),
                   jax.ShapeDtypeStruct((n_tok, LANES), I32),
                   jax.ShapeDtypeStruct((n_tok, LANES), F32),
                   jax.ShapeDtypeStruct((n_tok, LANES), I32),
                   jax.ShapeDtypeStruct((8, LANES), F32)],
        grid=(n_tok // tm,),
        in_specs=[pl.BlockSpec((tm, d), row),
                  pl.BlockSpec((1, d), lambda i: (0, 0)),
                  pl.BlockSpec((1, 1, d), bsel),
                  pl.BlockSpec((1, 1, d), bsel),
                  pl.BlockSpec((d, LANES), lambda i: (0, 0)),
                  pl.BlockSpec((1, LANES), lambda i: (0, 0))],
        out_specs=[pl.BlockSpec((tm, d // 2), row),
                   pl.BlockSpec((tm, LANES), row),
                   pl.BlockSpec((tm, LANES), row),
                   pl.BlockSpec((tm, LANES), row),
                   pl.BlockSpec((8, LANES), lambda i: (0, 0))],
        scratch_shapes=[pltpu.VMEM((1, LANES), F32)],
        compiler_params=_cparams(("arbitrary",)),
        name="moe_router",
    )(x2, g.reshape(1, d), scale.reshape(-1, 1, d), shift.reshape(-1, 1, d), rwp, rbp)


def _dispatch_kernel(dest_ref, h_ref, xs_in_ref, xs_ref, sem):
    del xs_in_ref
    tm = h_ref.shape[0]

    def row_copy(i, d):
        return pltpu.make_async_copy(h_ref.at[pl.ds(i, 1)], xs_ref.at[pl.ds(d, 1)], sem)

    def issue(i, c):
        for k in range(TOP_K):
            row_copy(i, dest_ref[i * TOP_K + k]).start()
        return c

    lax.fori_loop(0, tm, issue, 0)

    def drain(i, c):
        for k in range(TOP_K):
            row_copy(0, 0).wait()
        return c

    lax.fori_loop(0, tm, drain, 0)


def moe_dispatch(h32, dest_flat, rows, tm=256):
    n_tok, dw = h32.shape
    tm = min(tm, n_tok)
    xs0 = jnp.zeros((rows, dw), I32)
    return pl.pallas_call(
        _dispatch_kernel,
        out_shape=jax.ShapeDtypeStruct((rows, dw), I32),
        grid=(n_tok // tm,),
        in_specs=[pl.BlockSpec((tm * TOP_K,), lambda i: (i,), memory_space=pltpu.SMEM),
                  pl.BlockSpec((tm, dw), lambda i: (i, 0)),
                  pl.BlockSpec(memory_space=pl.ANY)],
        out_specs=pl.BlockSpec(memory_space=pl.ANY),
        scratch_shapes=[pltpu.SemaphoreType.DMA(())],
        input_output_aliases={2: 0},
        compiler_params=_cparams(("arbitrary",)),
        name="moe_dispatch",
    )(dest_flat, h32, xs0)


def _experts_kernel(ce_ref, nu_ref, x_ref, w1_ref, b1_ref, w2_ref, b2_ref, y_ref, act_scr, *, tn):
    c = pl.program_id(0)
    de = act_scr.shape[1]

    @pl.when(c < nu_ref[0])
    def _():
        words = x_ref[...]
        x = jnp.concatenate([pltpu.bitcast(words << 16, F32),
                             pltpu.bitcast(words & jnp.int32(-65536), F32)], axis=1).astype(BF16)
        for j in range(de // tn):
            sl = slice(j * tn, (j + 1) * tn)
            sl2 = slice(de + j * tn, de + (j + 1) * tn)
            glu = _dot(x, w1_ref[0, :, sl]) + b1_ref[0, :, sl]
            lin = _dot(x, w1_ref[0, :, sl2]) + b1_ref[0, :, sl2]
            glu = jnp.minimum(glu, SWIGLU_LIMIT)
            lin = jnp.clip(lin, -SWIGLU_LIMIT, SWIGLU_LIMIT)
            act_scr[:, sl] = (glu * jax.nn.sigmoid(SWIGLU_ALPHA * glu) * (lin + 1.0)).astype(BF16)
        a = act_scr[...]
        d = y_ref.shape[1]
        for j in range(d // tn):
            sl = slice(j * tn, (j + 1) * tn)
            y_ref[:, sl] = _dot(a, w2_ref[0, :, sl]) + b2_ref[0, :, sl]

    @pl.when(c >= nu_ref[0])
    def _():
        y_ref[...] = jnp.zeros_like(y_ref)


def moe_experts(xs, chunk_exp, n_used, w1, b1, w2, b2, tn=256):
    rows = xs.shape[0]
    ne, d, de2 = w1.shape
    de = de2 // 2
    n_chunks = rows // MOE_CHUNK
    grid_spec = pltpu.PrefetchScalarGridSpec(
        num_scalar_prefetch=2,
        grid=(n_chunks,),
        in_specs=[pl.BlockSpec((MOE_CHUNK, d // 2), lambda c, ce, nu: (c, 0)),
                  pl.BlockSpec((1, d, de2), lambda c, ce, nu: (ce[c], 0, 0)),
                  pl.BlockSpec((1, 1, de2), lambda c, ce, nu: (ce[c], 0, 0)),
                  pl.BlockSpec((1, de, d), lambda c, ce, nu: (ce[c], 0, 0)),
                  pl.BlockSpec((1, 1, d), lambda c, ce, nu: (ce[c], 0, 0))],
        out_specs=pl.BlockSpec((MOE_CHUNK, d), lambda c, ce, nu: (c, 0)),
        scratch_shapes=[pltpu.VMEM((MOE_CHUNK, de), BF16)],
    )
    return pl.pallas_call(
        functools.partial(_experts_kernel, tn=min(tn, de)),
        out_shape=jax.ShapeDtypeStruct((rows, d), F32),
        grid_spec=grid_spec,
        compiler_params=_cparams(("arbitrary",), VMEM_LIMIT_BYTES),
        name="moe_experts",
    )(chunk_exp, n_used, xs, w1, b1.reshape(ne, 1, de2), w2, b2.reshape(ne, 1, d))


def _combine_kernel(dest_ref, y_ref, tg_ref, x_ref, gate_ref, o_ref, buf, sem):
    tm = x_ref.shape[0]

    def row_copy(i, k, d):
        return pltpu.make_async_copy(y_ref.at[pl.ds(d, 1)], buf.at[k, pl.ds(i, 1)], sem)

    def issue(i, c):
        for k in range(TOP_K):
            row_copy(i, k, dest_ref[i * TOP_K + k]).start()
        return c

    lax.fori_loop(0, tm, issue, 0)

    def drain(i, c):
        for k in range(TOP_K):
            row_copy(0, 0, 0).wait()
        return c

    lax.fori_loop(0, tm, drain, 0)
    tg = tg_ref[...]
    moe = tg[:, 0:1] * buf[0]
    for k in range(1, TOP_K):
        moe = moe + tg[:, k:k + 1] * buf[k]
    o_ref[...] = x_ref[...] + gate_ref[0] * moe


def moe_combine(y, dest_flat, tg, x2, gate, seq, tm=256):
    n_tok, d = x2.shape
    tm = min(tm, seq)
    row = lambda i: (i, 0)
    return pl.pallas_call(
        _combine_kernel,
        out_shape=jax.ShapeDtypeStruct((n_tok, d), F32),
        grid=(n_tok // tm,),
        in_specs=[pl.BlockSpec((tm * TOP_K,), lambda i: (i,), memory_space=pltpu.SMEM),
                  pl.BlockSpec(memory_space=pl.ANY),
                  pl.BlockSpec((tm, LANES), row),
                  pl.BlockSpec((tm, d), row),
                  pl.BlockSpec((1, 1, d), lambda i: (i * tm // seq, 0, 0))],
        out_specs=pl.BlockSpec((tm, d), row),
        scratch_shapes=[pltpu.VMEM((TOP_K, tm, d), F32), pltpu.SemaphoreType.DMA(())],
        compiler_params=_cparams(("arbitrary",), VMEM_LIMIT_BYTES),
        name="moe_combine",
    )(dest_flat, y, tg, x2, gate.reshape(-1, 1, d))


def moe_block(x2, g, scale, shift, gate, rw, rb, w1, b1, w2, b2, seq):
    n_tok, d = x2.shape
    h, ti, tg, rk, cnt = moe_router(x2, g, scale, shift, rw, rb, seq)
    counts = cnt[0, :N_EXPERTS].astype(I32)
    padded = ((counts + MOE_CHUNK - 1) // MOE_CHUNK) * MOE_CHUNK
    ends_pad = jnp.cumsum(padded)
    start_pad = ends_pad - padded
    n_chunks = -(-(n_tok * TOP_K) // MOE_CHUNK) + N_EXPERTS
    rows = n_chunks * MOE_CHUNK
    experts = jnp.arange(N_EXPERTS, dtype=I32)
    slot_start = jnp.sum(jnp.where(ti[:, :TOP_K, None] == experts, start_pad, 0), axis=-1)
    dest = (slot_start + rk[:, :TOP_K]).reshape(-1)
    chunk_pos = jnp.arange(n_chunks, dtype=I32)[:, None] * MOE_CHUNK
    chunk_exp = jnp.minimum(jnp.sum((ends_pad[None, :] <= chunk_pos).astype(I32), axis=1), N_EXPERTS - 1)
    n_used = (ends_pad[-1:] // MOE_CHUNK).astype(I32)
    xs = moe_dispatch(h, dest, rows)
    y = moe_experts(xs, chunk_exp, n_used, w1.astype(BF16), b1, w2.astype(BF16), b2)
    return moe_combine(y, dest, tg, x2, gate, seq)


def _dsa_index_kernel(qi_ref, k2_ref, wi_ref, mask_ref, ka_scr, kb_scr, key_scr,
                      *, S, KC, n_sel, idx_scale):
    qb = pl.program_id(1)
    R = qi_ref.shape[1]
    nchunks = S // KC
    lane = lax.broadcasted_iota(I32, (1, LANES), 1)

    @pl.when(qb == 0)
    def _():
        kk = k2_ref[0]
        ka_scr[...] = jnp.where(lane < HEAD_DIM, kk, 0.0).astype(BF16)
        kb_scr[...] = jnp.where(lane >= HEAD_DIM, kk, 0.0).astype(BF16)

    nck = (qb * R + R + KC - 1) // KC
    w = wi_ref[0].astype(BF16).astype(F32)
    qv = qi_ref[0].astype(BF16)

    def relu_bf16(d):
        return jnp.maximum(d, 0.0).astype(BF16).astype(F32)

    row = qb * R + lax.broadcasted_iota(I32, (R, KC), 0)
    col0 = lax.broadcasted_iota(I32, (R, KC), 1)

    def score_chunk(c, carry):
        off = pl.multiple_of(c * KC, KC)
        ka = ka_scr[pl.ds(off, KC), :]
        kb = kb_scr[pl.ds(off, KC), :]
        sc = jnp.zeros((R, KC), F32)
        for pp in range(N_IDX_HEADS // 2):
            qp = qv[:, pp * LANES:(pp + 1) * LANES]
            sc = sc + w[:, 2 * pp:2 * pp + 1] * relu_bf16(_dot_nt(qp, ka))
            sc = sc + w[:, 2 * pp + 1:2 * pp + 2] * relu_bf16(_dot_nt(qp, kb))
        sc = sc * idx_scale
        sc = jnp.where(sc == 0.0, 0.0, sc)
        bits = pltpu.bitcast(sc, I32)
        key = jnp.where(bits < 0, bits ^ jnp.int32(0x7FFFFFFF), bits)
        key_scr[c] = jnp.where(off + col0 <= row, key, INT_MIN)
        return carry

    lax.fori_loop(0, nck, score_chunk, 0)

    def count(pred):
        def body(c, acc):
            kc = key_scr[c]
            for j in range(KC // LANES):
                acc = acc + jnp.where(pred(kc[:, j * LANES:(j + 1) * LANES], c * KC + j * LANES), 1.0, 0.0)
            return acc
        acc = lax.fori_loop(0, nck, body, jnp.zeros((R, LANES), F32))
        return jnp.broadcast_to(jnp.sum(acc, axis=1, keepdims=True), (R, LANES))

    nsel_f = float(n_sel)

    def bit_step(t, carry):
        cur, cnt_cur = carry
        cand = cur + (jnp.int32(1) << (31 - t))
        cnt = count(lambda kc, base: kc >= cand)
        ok = cnt >= nsel_f
        return jnp.where(ok, cand, cur), jnp.where(ok, cnt, cnt_cur)

    tau, cnt_tau = lax.fori_loop(
        0, 32, bit_step,
        (jnp.full((R, LANES), INT_MIN, I32), jnp.full((R, LANES), float(S), F32)))

    overflow = jnp.logical_and(tau > INT_MIN, cnt_tau > nsel_f)
    any_over = jnp.max(jnp.where(overflow, 1.0, 0.0)) > 0.0
    lane_r = lax.broadcasted_iota(I32, (R, LANES), 1)

    def write_mask(keep_fn):
        for c in range(nchunks):
            @pl.when(c < nck)
            def _():
                kc = key_scr[c]
                for j in range(KC // LANES):
                    keep = keep_fn(kc[:, j * LANES:(j + 1) * LANES], c * KC + j * LANES + lane_r)
                    mask_ref[0, :, c * KC + j * LANES:c * KC + (j + 1) * LANES] = (
                        jnp.where(keep, 1, 0).astype(jnp.int8))

            @pl.when(c >= nck)
            def _():
                mask_ref[0, :, c * KC:(c + 1) * KC] = jnp.zeros((R, KC), jnp.int8)

    @pl.when(jnp.logical_not(any_over))
    def _():
        floor = jnp.maximum(tau, INT_MIN + 1)
        write_mask(lambda kj, pos: kj >= floor)

    @pl.when(any_over)
    def _():
        need = nsel_f - count(lambda kc, base: kc > tau)

        def jstep(t, cur):
            cand = cur + (jnp.int32(1) << (S.bit_length() - 2 - t))
            cnt = count(lambda kc, base: jnp.logical_and(kc == tau, base + lane_r < cand))
            return jnp.where(cnt < need, cand, cur)

        jlim = lax.fori_loop(0, S.bit_length() - 1, jstep, jnp.zeros((R, LANES), I32))
        write_mask(lambda kj, pos: jnp.logical_and(
            jnp.logical_or(kj > tau, jnp.logical_and(kj == tau, pos <= jlim)), kj > INT_MIN))


def dsa_index_mask(aux, qcol, kcol, wcol, n_sel, R=128, KC=512):
    bsz, seq, _ = aux.shape
    R = min(R, seq)
    KC = min(KC, seq)
    nq = N_IDX_HEADS * HEAD_DIM
    return pl.pallas_call(
        functools.partial(_dsa_index_kernel, S=seq, KC=KC, n_sel=n_sel,
                          idx_scale=float(nq ** -0.5)),
        out_shape=jax.ShapeDtypeStruct((bsz, seq, seq), jnp.int8),
        grid=(bsz, seq // R),
        in_specs=[pl.BlockSpec((1, R, nq), lambda b, q: (b, q, qcol // nq)),
                  pl.BlockSpec((1, seq, LANES), lambda b, q: (b, 0, kcol // LANES)),
                  pl.BlockSpec((1, R, LANES), lambda b, q: (b, q, wcol // LANES))],
        out_specs=pl.BlockSpec((1, R, seq), lambda b, q: (b, q, 0)),
        scratch_shapes=[pltpu.VMEM((seq, LANES), BF16), pltpu.VMEM((seq, LANES), BF16),
                        pltpu.VMEM((seq // KC, R, KC), I32)],
        compiler_params=_cparams(("arbitrary", "arbitrary"), VMEM_LIMIT_BYTES),
        name="dsa_index_mask",
    )(aux, aux, aux)


def _dsa_attn_kernel(qi_ref, ki_ref, q_ref, k_ref, v_ref, mask_ref, tz_ref, o_ref,
                     m_scr, l_scr, acc_scr, *, TQ, TK, DCL):
    s = pl.program_id(1)
    qi = qi_ref[s]
    ki = ki_ref[s]
    k_last = ((qi + 1) * TQ - 1) // TK

    @pl.when(ki == 0)
    def _():
        m_scr[...] = jnp.full_like(m_scr, NEG)
        l_scr[...] = jnp.zeros_like(l_scr)
        acc_scr[...] = jnp.zeros_like(acc_scr)

    lane = lax.broadcasted_iota(I32, (1, LANES), 1)
    lo_half = lane < HEAD_DIM
    d0 = qi * TQ - ki * TK
    cb0 = (DCL - jnp.minimum(d0, DCL)) // LANES
    madd = jnp.where(mask_ref[0].astype(I32) != 0, 0.0, NEG).astype(BF16)
    npair = q_ref.shape[2] // LANES

    def logits(h):
        cs = slice((h // 2) * LANES, (h // 2 + 1) * LANES)
        qsel = lo_half if h % 2 == 0 else jnp.logical_not(lo_half)
        q = _keep_lanes(q_ref[0, :, cs] * QK_SCALE, qsel)
        bias = jnp.concatenate([tz_ref[h, cb0 + jj] for jj in range(TK // LANES)], axis=1)
        return _dot_nt(q, k_ref[0, :, cs]).astype(BF16) + bias + madd

    def weighted_values(h, pm):
        return _dot(pm, v_ref[0, :, h * LANES:(h + 1) * LANES])

    def accumulate(pp, alphas, pvs):
        cs = slice(pp * LANES, (pp + 1) * LANES)
        acc_scr[:, cs] = (acc_scr[:, cs] * jnp.where(lo_half, alphas[0], alphas[1])
                          + jnp.where(lo_half, pvs[0], pvs[1]))
        l_scr[pp] = (l_scr[pp] * jnp.where(lo_half, alphas[1], alphas[0])
                     + jnp.where(lo_half, pvs[1], pvs[0]))

    ahead = 2
    nh = 2 * npair
    queue = [logits(h) for h in range(ahead)]
    alphas, pvs, probs = {}, {}, {}
    for h in range(nh + 1):
        if h < nh:
            sc = queue.pop(0)
            if h + ahead < nh:
                queue.append(logits(h + ahead))
            m_scr[h], alphas[h], probs[h] = _online_softmax(sc, m_scr[h])
        if h >= 1:
            pvs[h - 1] = weighted_values(h - 1, probs.pop(h - 1))
            if (h - 1) % 2 == 1:
                accumulate((h - 1) // 2, (alphas[h - 2], alphas[h - 1]), (pvs[h - 2], pvs[h - 1]))

    @pl.when(ki == k_last)
    def _():
        for pp in range(npair):
            cs = slice(pp * LANES, (pp + 1) * LANES)
            l = pltpu.roll(l_scr[pp], HEAD_DIM, axis=1)
            o_ref[0, :, cs] = (acc_scr[:, cs] / l).astype(o_ref.dtype)


def _dsa_bias_table(rel_bias, seq, TQ, TK, DCL):
    i = np.arange(TQ)[:, None]
    c = np.arange(DCL + TK)[None, :]
    d = np.clip(i - c + DCL, 0, seq - 1)
    tz = _bucket_bias(jnp.asarray(d, I32), rel_bias)
    tz = tz.reshape(rel_bias.shape[1], TQ, (DCL + TK) // LANES, LANES)
    return tz.transpose(0, 2, 1, 3).astype(BF16)


def _np_t5_bucket(d):
    max_exact = N_BUCKETS // 2
    df = np.maximum(d, 1).astype(np.float32)
    large = max_exact + (np.log(df / max_exact) / math.log(MAX_DISTANCE / max_exact)
                         * (N_BUCKETS - max_exact)).astype(np.int32)
    return np.where(d < max_exact, d, np.minimum(large, N_BUCKETS - 1))


def dsa_attention(q_src, qcol, kv, mask, rel_bias, TQ=256, TK=1024):
    bsz, seq, _ = q_src.shape
    TQ = min(TQ, seq)
    TK = min(TK, seq)
    hw = N_DSA_HEADS * HEAD_DIM
    buckets = _np_t5_bucket(np.arange(seq))
    first_const = int(np.max(np.nonzero(buckets != buckets[-1])[0])) + 1
    DCL = min(seq, -(-(first_const + TK - 1) // LANES) * LANES)
    tz = _dsa_bias_table(rel_bias, seq, TQ, TK, DCL)
    qs, ks = _tri_schedule(seq // TQ, TK // TQ)
    grid_spec = pltpu.PrefetchScalarGridSpec(
        num_scalar_prefetch=2,
        grid=(bsz, qs.shape[0]),
        in_specs=[
            pl.BlockSpec((1, TQ, hw), lambda b, s, qi, ki: (b, qi[s], qcol // hw)),
            pl.BlockSpec((1, TK, hw), lambda b, s, qi, ki: (b, ki[s], 0)),
            pl.BlockSpec((1, TK, 2 * hw), lambda b, s, qi, ki: (b, ki[s], 0)),
            pl.BlockSpec((1, TQ, TK), lambda b, s, qi, ki: (b, qi[s], ki[s])),
            pl.BlockSpec(tz.shape, lambda b, s, qi, ki: (0, 0, 0, 0), pipeline_mode=pl.Buffered(1)),
        ],
        out_specs=pl.BlockSpec((1, TQ, hw), lambda b, s, qi, ki: (b, qi[s], 0)),
        scratch_shapes=[pltpu.VMEM((N_DSA_HEADS, TQ, LANES), F32),
                        pltpu.VMEM((N_DSA_HEADS // 2, TQ, LANES), F32),
                        pltpu.VMEM((TQ, hw), F32)],
    )
    return pl.pallas_call(
        functools.partial(_dsa_attn_kernel, TQ=TQ, TK=TK, DCL=DCL),
        out_shape=jax.ShapeDtypeStruct((bsz, seq, hw), BF16),
        grid_spec=grid_spec,
        compiler_params=_cparams(("arbitrary", "arbitrary"), VMEM_LIMIT_BYTES),
        name="dsa_attention",
    )(qs, ks, q_src, kv, _values_with_ones(kv[:, :, hw:], N_DSA_HEADS), mask, tz)


def _rms_kernel(x_ref, g_ref, o_ref):
    x = x_ref[...]
    ms = jnp.mean(x * x, axis=-1, keepdims=True)
    o_ref[...] = x * lax.rsqrt(ms + EPS) * g_ref[...]


def final_rmsnorm(x2, g, tm=512):
    n_tok, d = x2.shape
    tm = min(tm, n_tok)
    return pl.pallas_call(
        _rms_kernel,
        out_shape=jax.ShapeDtypeStruct((n_tok, d), F32),
        grid=(n_tok // tm,),
        in_specs=[pl.BlockSpec((tm, d), lambda i: (i, 0)), pl.BlockSpec((1, d), lambda i: (0, 0))],
        out_specs=pl.BlockSpec((tm, d), lambda i: (i, 0)),
        compiler_params=_cparams(("arbitrary",)),
        name="final_rmsnorm",
    )(x2, g.reshape(1, d))


def even_mixer_residual(x2, shift1, scale1, gate1, norm1, w_in, fox_fb, w_out, rel_bias, bsz, seq):
    d = x2.shape[1]
    nfq = 3 * N_FOX_HEADS * HEAD_DIM
    w_main = jnp.concatenate([w_in[:, :nfq], w_in[:, nfq + N_FOX_HEADS:]], axis=1).astype(BF16)
    rep = np.repeat(np.arange(N_FOX_HEADS), FOX_PARTS)
    gcols = np.concatenate([rep, rep])
    w_gate = jnp.zeros((d, LANES), F32).at[:, :2 * FOX_XW].set(w_in[:, nfq + gcols])
    fb = jnp.zeros((1, LANES), F32).at[0, :2 * FOX_XW].set(fox_fb[gcols])
    proj, gate_z = norm_mod_matmul(x2, norm1, scale1, shift1, w_main, w_gate, seq)
    c = proj.shape[1]
    proj3 = proj.reshape(bsz, seq, c)
    qx, kx = fox_gate_columns(gate_z.reshape(bsz, seq, LANES), fb)
    fox = fox_attention(proj3, qx, kx, N_FOX_HEADS)
    dil_o, dil_l = [], []
    for g, (window, dil) in enumerate(DIL_PAIRS):
        bias = _dil_bias_tiles(rel_bias, g, window, dil)
        o, l = dilated_group_attention(proj3, bias, g, dil, nfq + g * N_DIL_SLOTS * HEAD_DIM)
        dil_o.append(o)
        dil_l.append(l)
    return out_proj_even(fox.reshape(bsz * seq, -1), dil_o, dil_l, w_out.astype(BF16), x2, gate1, seq)


def odd_mixer_residual(x2, shift1, scale1, gate1, norm1, w_in, kv_norm, w_ukv, w_out, rel_bias, bsz, seq):
    d = x2.shape[1]
    nq = N_DSA_HEADS * HEAD_DIM
    ni = N_IDX_HEADS * HEAD_DIM
    c_ckv, c_qi = nq, nq + KV_RANK
    c_ki = c_qi + ni
    c_wi = c_ki + HEAD_DIM
    w_main = w_in[:, :nq].astype(BF16)
    c_w, c_k, c_q = KV_RANK, KV_RANK + LANES, KV_RANK + 2 * LANES
    w_aux = jnp.zeros((d, c_q + ni), F32)
    w_aux = w_aux.at[:, :KV_RANK].set(w_in[:, c_ckv:c_qi])
    w_aux = w_aux.at[:, c_w:c_w + N_IDX_HEADS].set(w_in[:, c_wi:])
    w_aux = w_aux.at[:, c_k:c_k + HEAD_DIM].set(w_in[:, c_ki:c_wi])
    w_aux = w_aux.at[:, c_k + HEAD_DIM:c_q].set(w_in[:, c_ki:c_wi])
    w_aux = w_aux.at[:, c_q:].set(w_in[:, c_qi:c_ki])
    proj, aux = norm_mod_matmul(x2, norm1, scale1, shift1, w_main, w_aux, seq)
    zeros = jnp.zeros((bsz, KV_RANK), F32)
    kv = norm_mod_matmul(aux[:, :KV_RANK], kv_norm, zeros, zeros, w_ukv.astype(BF16), None, seq)
    proj3 = proj.reshape(bsz, seq, -1)
    n_sel = min(DSA_TOPK, seq // 4)
    mask = dsa_index_mask(aux.reshape(bsz, seq, -1), c_q, c_k, c_w, n_sel)
    att = dsa_attention(proj3, 0, kv.reshape(bsz, seq, -1), mask, rel_bias)
    return out_proj_odd(att.reshape(bsz * seq, -1), w_out.astype(BF16), x2, gate1, seq)


def kernel(x, c, rel_bias, l0_norm1, l0_ada_w, l0_ada_b, l0_w_in, l0_fox_fb, l0_w_out, l0_norm2, l0_router_w, l0_router_b, l0_w1, l0_b1, l0_w2, l0_b2, l1_norm1, l1_ada_w, l1_ada_b, l1_w_in, l1_kv_norm, l1_w_ukv, l1_w_out, l1_norm2, l1_router_w, l1_router_b, l1_w1, l1_b1, l1_w2, l1_b2, final_norm):
    bsz, seq, d = x.shape
    x2 = x.reshape(bsz * seq, d)
    layers = (
        (l0_norm1, l0_ada_w, l0_ada_b, l0_norm2, l0_router_w, l0_router_b, l0_w1, l0_b1, l0_w2, l0_b2),
        (l1_norm1, l1_ada_w, l1_ada_b, l1_norm2, l1_router_w, l1_router_b, l1_w1, l1_b1, l1_w2, l1_b2),
    )
    for i, (norm1, ada_w, ada_b, norm2, rw, rb, w1, b1, w2, b2) in enumerate(layers):
        mods = ada_mods(c, ada_w, ada_b)
        shift1, scale1, gate1, shift2, scale2, gate2 = (mods[:, j * d:(j + 1) * d] for j in range(6))
        if i % 2 == 0:
            x2 = even_mixer_residual(x2, shift1, scale1, gate1, norm1, l0_w_in, l0_fox_fb, l0_w_out,
                                     rel_bias, bsz, seq)
        else:
            x2 = odd_mixer_residual(x2, shift1, scale1, gate1, norm1, l1_w_in, l1_kv_norm, l1_w_ukv,
                                    l1_w_out, rel_bias, bsz, seq)
        x2 = moe_block(x2, norm2, scale2, shift2, gate2, rw, rb, w1, b1, w2, b2, seq)
    return final_rmsnorm(x2, final_norm).reshape(bsz, seq, d)
```

```python
import functools
import math

import numpy as np
import jax
import jax.numpy as jnp
from jax import lax
from jax.experimental import pallas as pl
from jax.experimental.pallas import tpu as pltpu

F32, BF16, I32 = jnp.float32, jnp.bfloat16, jnp.int32

LANES = 128
VMEM_LIMIT_BYTES = 56 * 1024 * 1024

HEAD_DIM = 64
N_FOX_HEADS = 8
DIL_PAIRS = ((128, 1), (512, 4), (2048, 16))
N_DIL_SLOTS = 4
N_DSA_HEADS = 16
KV_RANK = 256
N_IDX_HEADS = 8
DSA_TOPK = 256
N_BUCKETS = 32
MAX_DISTANCE = 2048
N_EXPERTS = 32
TOP_K = 4
D_EXPERT = 1024
SWIGLU_ALPHA = 1.702
SWIGLU_LIMIT = 7.0
MOE_CHUNK = 256
EPS = 1e-6

NEG = -1e30
INT_MIN = -(2 ** 31)
QK_SCALE = HEAD_DIM ** -0.5
FOX_PARTS = 3
FOX_XW = FOX_PARTS * N_FOX_HEADS


def _cparams(sem, vmem=None):
    return pltpu.CompilerParams(dimension_semantics=sem, vmem_limit_bytes=vmem)


def _dot_nt(a, b):
    return lax.dot_general(a, b, (((1,), (1,)), ((), ())), preferred_element_type=F32)


def _dot(a, b):
    return jnp.dot(a, b, preferred_element_type=F32)


def _dot_f32(a, b):
    return jnp.dot(a, b, precision=lax.Precision.HIGHEST, preferred_element_type=F32)


def _keep_lanes(x, sel):
    return jnp.where(sel, x.astype(F32), 0.0).astype(x.dtype)


def _online_softmax(sc, m_prev):
    cols = [sc[:, j:j + LANES] for j in range(0, sc.shape[1], LANES)]
    m_cur = functools.reduce(jnp.maximum, cols).astype(F32)
    m_new = jnp.maximum(m_prev, jnp.max(m_cur, axis=1, keepdims=True))
    alpha = jnp.exp(m_prev - m_new)
    shift = m_new.astype(BF16)
    return m_new, alpha, jnp.concatenate([jnp.exp(c - shift) for c in cols], axis=1)


def _values_with_ones(v, n_heads):
    lead = v.shape[:-1]
    v4 = v.reshape(*lead, n_heads // 2, 2, HEAD_DIM)
    ones = jnp.ones((*lead, n_heads // 2, HEAD_DIM), v.dtype)
    va = jnp.concatenate([v4[..., 0, :], ones], axis=-1)
    vb = jnp.concatenate([ones, v4[..., 1, :]], axis=-1)
    return jnp.stack([va, vb], axis=-2).reshape(*lead, n_heads * 2 * HEAD_DIM)


def _tri_schedule(nq, kq_ratio=1):
    qs, ks = [], []
    for qi in range(nq):
        for ki in range(qi // kq_ratio + 1):
            qs.append(qi)
            ks.append(ki)
    return jnp.asarray(qs, I32), jnp.asarray(ks, I32)


def _ada_kernel(c_ref, w_ref, b_ref, o_ref):
    c = c_ref[...]
    sc = c * jax.nn.sigmoid(c)
    o_ref[...] = _dot_f32(sc, w_ref[...]) + b_ref[...]


def ada_mods(c, w, b):
    bsz, d = c.shape
    n = w.shape[1]
    tn = min(n, 1024)
    cp = jnp.zeros((8, d), F32).at[:bsz].set(c)
    out = pl.pallas_call(
        _ada_kernel,
        out_shape=jax.ShapeDtypeStruct((8, n), F32),
        grid=(n // tn,),
        in_specs=[pl.BlockSpec((8, d), lambda j: (0, 0)),
                  pl.BlockSpec((d, tn), lambda j: (0, j)),
                  pl.BlockSpec((1, tn), lambda j: (0, j))],
        out_specs=pl.BlockSpec((8, tn), lambda j: (0, j)),
        compiler_params=_cparams(("arbitrary",)),
        name="ada_mods",
    )(cp, w, b.reshape(1, n))
    return out[:bsz]


def _nmm_kernel(x_ref, g_ref, sc_ref, sh_ref, w_ref, *rest, tn, has_aux):
    if has_aux:
        wa_ref, o_ref, oa_ref = rest
    else:
        (o_ref,) = rest
    x = x_ref[...]
    ms = jnp.mean(x * x, axis=-1, keepdims=True)
    y = x * lax.rsqrt(ms + EPS) * g_ref[...]
    hf = y * (1.0 + sc_ref[0]) + sh_ref[0]
    h = hf.astype(BF16)
    n = o_ref.shape[1]
    for j0 in range(0, n, tn):
        j1 = min(j0 + tn, n)
        o_ref[:, j0:j1] = _dot(h, w_ref[:, j0:j1]).astype(o_ref.dtype)
    if has_aux:
        na = oa_ref.shape[1]
        for j0 in range(0, na, tn):
            j1 = min(j0 + tn, na)
            oa_ref[:, j0:j1] = _dot_f32(hf, wa_ref[:, j0:j1])


def norm_mod_matmul(x2, g, scale, shift, w, w_aux, seq, tm=512, tn=256):
    n_tok, din = x2.shape
    nout = w.shape[1]
    tm = min(tm, seq)
    has_aux = w_aux is not None
    bsel = lambda i: (i * tm // seq, 0, 0)
    in_specs = [pl.BlockSpec((tm, din), lambda i: (i, 0)),
                pl.BlockSpec((1, din), lambda i: (0, 0)),
                pl.BlockSpec((1, 1, din), bsel),
                pl.BlockSpec((1, 1, din), bsel),
                pl.BlockSpec((din, nout), lambda i: (0, 0))]
    out_shape = [jax.ShapeDtypeStruct((n_tok, nout), BF16)]
    out_specs = [pl.BlockSpec((tm, nout), lambda i: (i, 0))]
    args = [x2, g.reshape(1, din), scale.reshape(-1, 1, din), shift.reshape(-1, 1, din), w]
    if has_aux:
        na = w_aux.shape[1]
        in_specs.append(pl.BlockSpec((din, na), lambda i: (0, 0)))
        out_shape.append(jax.ShapeDtypeStruct((n_tok, na), F32))
        out_specs.append(pl.BlockSpec((tm, na), lambda i: (i, 0)))
        args.append(w_aux)
    outs = pl.pallas_call(
        functools.partial(_nmm_kernel, tn=min(tn, nout), has_aux=has_aux),
        out_shape=out_shape,
        grid=(n_tok // tm,),
        in_specs=in_specs,
        out_specs=out_specs,
        compiler_params=_cparams(("arbitrary",), VMEM_LIMIT_BYTES),
        name="norm_mod_matmul",
    )(*args)
    return outs if has_aux else outs[0]


def _foxcum_kernel(z_ref, fb_ref, qx_ref, kx_ref, carry):
    t = pl.program_id(1)

    @pl.when(t == 0)
    def _():
        carry[...] = jnp.zeros_like(carry)

    ts = z_ref.shape[1]
    z = z_ref[0] + fb_ref[...]
    lf = jnp.minimum(z, 0.0) - jnp.log1p(jnp.exp(-jnp.abs(z)))
    row = lax.broadcasted_iota(I32, (ts, ts), 0)
    col = lax.broadcasted_iota(I32, (ts, ts), 1)
    tri = jnp.where(row >= col, 1.0, 0.0).astype(F32)
    c = jnp.dot(tri, lf, precision=lax.Precision.HIGHEST, preferred_element_type=F32) + carry[...]
    carry[...] = c[ts - 1:ts, :]
    hi = c.astype(BF16).astype(F32)
    r1 = c - hi
    mid = r1.astype(BF16).astype(F32)
    lo = (r1 - mid).astype(BF16).astype(F32)
    lane = lax.broadcasted_iota(I32, (ts, LANES), 1)
    ph = lane % FOX_PARTS
    part = jnp.where(ph == 0, hi, jnp.where(ph == 1, mid, lo))
    first = lane < FOX_XW
    second = (lane >= FOX_XW) & (lane < 2 * FOX_XW)
    qx_ref[0] = jnp.where(first, part, jnp.where(second, 1.0, 0.0)).astype(BF16)
    kx_ref[0] = jnp.where(first, 1.0, jnp.where(second, -part, 0.0)).astype(BF16)


def fox_gate_columns(z, fb, ts=512):
    bsz, seq, _ = z.shape
    ts = min(ts, seq)
    return pl.pallas_call(
        _foxcum_kernel,
        out_shape=[jax.ShapeDtypeStruct((bsz, seq, LANES), BF16)] * 2,
        grid=(bsz, seq // ts),
        in_specs=[pl.BlockSpec((1, ts, LANES), lambda b, t: (b, t, 0)),
                  pl.BlockSpec((1, LANES), lambda b, t: (0, 0))],
        out_specs=[pl.BlockSpec((1, ts, LANES), lambda b, t: (b, t, 0))] * 2,
        scratch_shapes=[pltpu.VMEM((1, LANES), F32)],
        compiler_params=_cparams(("arbitrary", "arbitrary")),
        name="fox_gate_columns",
    )(z, fb)


def _fox_kernel(qi_ref, ki_ref, q_ref, k_ref, v_ref, qx_ref, kx_ref, o_ref, m_scr, l_scr, acc_scr, *, T):
    p = pl.program_id(1)
    s = pl.program_id(2)
    qi = qi_ref[s]
    ki = ki_ref[s]

    @pl.when(ki == 0)
    def _():
        m_scr[...] = jnp.full_like(m_scr, NEG)
        l_scr[...] = jnp.zeros_like(l_scr)
        acc_scr[...] = jnp.zeros_like(acc_scr)

    lane = lax.broadcasted_iota(I32, (1, LANES), 1)
    lo_half = lane < HEAD_DIM

    def step(masked):
        q = q_ref[0] * QK_SCALE
        qx = qx_ref[0]
        kcat = jnp.concatenate([k_ref[0], kx_ref[0]], axis=1)
        if masked:
            row = lax.broadcasted_iota(I32, (T, T), 0)
            col = lax.broadcasted_iota(I32, (T, T), 1)
            causal_add = jnp.where(col <= row, 0.0, NEG).astype(BF16)
        scs = []
        for hh in range(2):
            xl = FOX_PARTS * (2 * p + hh)
            qsel = lo_half if hh == 0 else jnp.logical_not(lo_half)
            xsel = ((lane >= xl) & (lane < xl + FOX_PARTS)) | (
                (lane >= FOX_XW + xl) & (lane < FOX_XW + xl + FOX_PARTS))
            qcat = jnp.concatenate([_keep_lanes(q, qsel), _keep_lanes(qx, xsel)], axis=1)
            sc = _dot_nt(qcat, kcat).astype(BF16)
            scs.append(sc + causal_add if masked else sc)
        pvs, alphas = [], []
        for hh in range(2):
            m_scr[hh], alpha, pm = _online_softmax(scs[hh], m_scr[hh])
            pvs.append(_dot(pm, v_ref[0, :, hh * LANES:(hh + 1) * LANES]))
            alphas.append(alpha)
        acc_scr[...] = (acc_scr[...] * jnp.where(lo_half, alphas[0], alphas[1])
                        + jnp.where(lo_half, pvs[0], pvs[1]))
        l_scr[...] = (l_scr[...] * jnp.where(lo_half, alphas[1], alphas[0])
                      + jnp.where(lo_half, pvs[1], pvs[0]))

    @pl.when(ki < qi)
    def _():
        step(False)

    @pl.when(ki == qi)
    def _():
        step(True)
        l = pltpu.roll(l_scr[...], HEAD_DIM, axis=1)
        o_ref[0] = (acc_scr[...] / l).astype(o_ref.dtype)


def fox_attention(proj, qx, kx, n_heads, T=1024):
    bsz, seq, _ = proj.shape
    T = min(T, seq)
    npair = n_heads // 2
    hw = n_heads * HEAD_DIM
    v_ones = _values_with_ones(proj[:, :, 2 * hw:3 * hw], n_heads)
    qs, ks = _tri_schedule(seq // T)
    grid_spec = pltpu.PrefetchScalarGridSpec(
        num_scalar_prefetch=2,
        grid=(bsz, npair, qs.shape[0]),
        in_specs=[
            pl.BlockSpec((1, T, LANES), lambda b, p, s, qi, ki: (b, qi[s], p)),
            pl.BlockSpec((1, T, LANES), lambda b, p, s, qi, ki: (b, ki[s], npair + p)),
            pl.BlockSpec((1, T, 2 * LANES), lambda b, p, s, qi, ki: (b, ki[s], p)),
            pl.BlockSpec((1, T, LANES), lambda b, p, s, qi, ki: (b, qi[s], 0)),
            pl.BlockSpec((1, T, LANES), lambda b, p, s, qi, ki: (b, ki[s], 0)),
        ],
        out_specs=pl.BlockSpec((1, T, LANES), lambda b, p, s, qi, ki: (b, qi[s], p)),
        scratch_shapes=[pltpu.VMEM((2, T, LANES), F32), pltpu.VMEM((T, LANES), F32),
                        pltpu.VMEM((T, LANES), F32)],
    )
    return pl.pallas_call(
        functools.partial(_fox_kernel, T=T),
        out_shape=jax.ShapeDtypeStruct((bsz, seq, npair * LANES), BF16),
        grid_spec=grid_spec,
        compiler_params=_cparams(("arbitrary", "arbitrary", "arbitrary"), VMEM_LIMIT_BYTES),
        name="fox_attention",
    )(qs, ks, proj, proj, v_ones, qx, kx)


def _dil_kernel(q_ref, kp_ref, kc_ref, vp_ref, vc_ref, bias_ref, o_ref, lse_ref, *, W):
    ut = pl.program_id(3)
    lane = lax.broadcasted_iota(I32, (1, LANES), 1)
    lo_half = lane < HEAD_DIM
    q = q_ref[0] * QK_SCALE
    k = jnp.concatenate([kp_ref[0], kc_ref[0]], axis=0)
    v = jnp.concatenate([vp_ref[0], vc_ref[0]], axis=0)
    col = lax.broadcasted_iota(I32, (W, 2 * W), 1)
    no_prev = jnp.logical_and(ut == 0, col < W)
    outs, lses = [], []
    for hh in range(2):
        qsel = lo_half if hh == 0 else jnp.logical_not(lo_half)
        sc = _dot_nt(_keep_lanes(q, qsel), k) + bias_ref[0, hh]
        sc = jnp.where(no_prev, NEG, sc)
        mx = jnp.max(sc, axis=1, keepdims=True)
        e = jnp.exp(sc - mx)
        den = jnp.sum(e, axis=1, keepdims=True)
        outs.append(_dot((e / den).astype(BF16), v))
        lses.append(mx + jnp.log(den))
    o_ref[0] = jnp.where(lo_half, outs[0], outs[1])
    lse_ref[0] = jnp.where(lo_half, lses[0], lses[1])


def dilated_group_attention(proj, bias, g, dil, col0, W=128):
    bsz, seq, c = proj.shape
    sub = seq // dil
    pv = proj.reshape(bsz, sub, dil * c)
    cb = c // LANES
    qb = col0 // LANES
    kb = qb + 3 * N_DIL_SLOTS * HEAD_DIM // LANES
    vb = kb + 3 * N_DIL_SLOTS * HEAD_DIM // LANES
    prev = lambda u: jnp.maximum(u - 1, 0)
    out = pl.pallas_call(
        functools.partial(_dil_kernel, W=W),
        out_shape=[jax.ShapeDtypeStruct((bsz, sub, dil * 2 * LANES), F32)] * 2,
        grid=(bsz, dil, 2, sub // W),
        in_specs=[
            pl.BlockSpec((1, W, LANES), lambda b, a, p, u: (b, u, a * cb + qb + p)),
            pl.BlockSpec((1, W, LANES), lambda b, a, p, u: (b, prev(u), a * cb + kb + p)),
            pl.BlockSpec((1, W, LANES), lambda b, a, p, u: (b, u, a * cb + kb + p)),
            pl.BlockSpec((1, W, LANES), lambda b, a, p, u: (b, prev(u), a * cb + vb + p)),
            pl.BlockSpec((1, W, LANES), lambda b, a, p, u: (b, u, a * cb + vb + p)),
            pl.BlockSpec((1, 2, W, 2 * W), lambda b, a, p, u: (p, 0, 0, 0)),
        ],
        out_specs=[pl.BlockSpec((1, W, LANES), lambda b, a, p, u: (b, u, a * 2 + p))] * 2,
        compiler_params=_cparams(("arbitrary",) * 4),
        name=f"dilated_attention_g{g}",
    )(pv, pv, pv, pv, pv, bias)
    return [o.reshape(bsz * seq, 2 * LANES) for o in out]


def _t5_bucket(dist):
    max_exact = N_BUCKETS // 2
    d = jnp.maximum(dist, 0)
    df = jnp.maximum(d, 1).astype(F32)
    large = max_exact + (jnp.log(df / max_exact) / math.log(MAX_DISTANCE / max_exact)
                         * (N_BUCKETS - max_exact)).astype(I32)
    large = jnp.minimum(large, N_BUCKETS - 1)
    return jnp.where(d < max_exact, d, large)


def _bucket_bias(dist, table):
    onehot = (_t5_bucket(dist)[..., None] == jnp.arange(N_BUCKETS, dtype=I32)).astype(F32)
    out = jnp.einsum('...b,bh->h...', onehot, table.astype(F32), precision=lax.Precision.HIGHEST)
    return out


def _dil_bias_tiles(rel_bias, g, window, dil, W=128):
    assert window // dil == W
    i = np.arange(W)[:, None]
    j = np.arange(2 * W)[None, :]
    n = i + W - j
    valid = (n >= 0) & (n <= W)
    heads = rel_bias[:, g * N_DIL_SLOTS:(g + 1) * N_DIL_SLOTS]
    tiles = jnp.where(valid[None], _bucket_bias(jnp.asarray(np.clip(n, 0, W) * dil, I32), heads), NEG)
    return tiles.reshape(2, 2, W, 2 * W)


def _oproj0_kernel(fox_ref, o0_ref, o1_ref, o2_ref, l0_ref, l1_ref, l2_ref, w_ref, x_ref, gate_ref, out_ref):
    ls = [l0_ref[...], l1_ref[...], l2_ref[...]]
    os_ = [o0_ref[...], o1_ref[...], o2_ref[...]]
    mx = jnp.maximum(jnp.maximum(ls[0], ls[1]), ls[2])
    ws = [jnp.exp(l - mx) for l in ls]
    den = ws[0] + ws[1] + ws[2]
    dil = (ws[0] * os_[0] + ws[1] * os_[1] + ws[2] * os_[2]) / den
    nf = fox_ref.shape[1]
    mix = _dot(fox_ref[...], w_ref[:nf, :]) + _dot(dil.astype(BF16), w_ref[nf:, :])
    out_ref[...] = x_ref[...] + gate_ref[0] * mix


def out_proj_even(fox, dil_o, dil_l, w, x2, gate, seq, tm=512):
    n_tok, d = x2.shape
    tm = min(tm, seq)
    nf = fox.shape[1]
    nd = N_DIL_SLOTS * HEAD_DIM
    row = lambda i: (i, 0)
    return pl.pallas_call(
        _oproj0_kernel,
        out_shape=jax.ShapeDtypeStruct((n_tok, d), F32),
        grid=(n_tok // tm,),
        in_specs=[pl.BlockSpec((tm, nf), row)]
                 + [pl.BlockSpec((tm, nd), row)] * 6
                 + [pl.BlockSpec((nf + nd, d), lambda i: (0, 0)),
                    pl.BlockSpec((tm, d), row),
                    pl.BlockSpec((1, 1, d), lambda i: (i * tm // seq, 0, 0))],
        out_specs=pl.BlockSpec((tm, d), row),
        compiler_params=_cparams(("arbitrary",), VMEM_LIMIT_BYTES),
        name="out_proj_even",
    )(fox, *dil_o, *dil_l, w, x2, gate.reshape(-1, 1, d))


def _oproj_kernel(a_ref, w_ref, x_ref, gate_ref, out_ref):
    out_ref[...] = x_ref[...] + gate_ref[0] * _dot(a_ref[...], w_ref[...])


def out_proj_odd(a, w, x2, gate, seq, tm=512):
    n_tok, d = x2.shape
    tm = min(tm, seq)
    ka = a.shape[1]
    row = lambda i: (i, 0)
    return pl.pallas_call(
        _oproj_kernel,
        out_shape=jax.ShapeDtypeStruct((n_tok, d), F32),
        grid=(n_tok // tm,),
        in_specs=[pl.BlockSpec((tm, ka), row),
                  pl.BlockSpec((ka, d), lambda i: (0, 0)),
                  pl.BlockSpec((tm, d), row),
                  pl.BlockSpec((1, 1, d), lambda i: (i * tm // seq, 0, 0))],
        out_specs=pl.BlockSpec((tm, d), row),
        compiler_params=_cparams(("arbitrary",), VMEM_LIMIT_BYTES),
        name="out_proj_odd",
    )(a, w, x2, gate.reshape(-1, 1, d))


def _router_kernel(x_ref, g_ref, sc_ref, sh_ref, rw_ref, rb_ref,
                   h_ref, ti_ref, tg_ref, rk_ref, cnt_ref, carry):
    i = pl.program_id(0)

    @pl.when(i == 0)
    def _():
        carry[...] = jnp.zeros_like(carry)

    tm = x_ref.shape[0]
    x = x_ref[...]
    ms = jnp.mean(x * x, axis=-1, keepdims=True)
    y = x * lax.rsqrt(ms + EPS) * g_ref[...]
    hr = (y * (1.0 + sc_ref[0]) + sh_ref[0]).astype(BF16).astype(F32)
    bits = pltpu.bitcast(hr, I32)
    half = hr.shape[1] // 2
    h_ref[...] = lax.shift_right_logical(bits[:, :half], 16) | (bits[:, half:] & jnp.int32(-65536))
    logits = _dot_f32(hr, rw_ref[...]) + rb_ref[...]
    lane = lax.broadcasted_iota(I32, (tm, LANES), 1)
    l = logits
    vals, idxs, hots = [], [], []
    for _ in range(TOP_K):
        mx = jnp.max(l, axis=1, keepdims=True)
        idx = jnp.min(jnp.where(l == mx, lane, LANES), axis=1, keepdims=True)
        hot = lane == idx
        l = jnp.where(hot, -jnp.inf, l)
        vals.append(mx)
        idxs.append(idx)
        hots.append(hot)
    es = [jnp.exp(v - vals[0]) for v in vals]
    den = es[0] + es[1] + es[2] + es[3]
    member = jnp.zeros((tm, LANES), F32)
    for hot in hots:
        member = member + jnp.where(hot, 1.0, 0.0)
    row = lax.broadcasted_iota(I32, (tm, tm), 0)
    col = lax.broadcasted_iota(I32, (tm, tm), 1)
    tri = jnp.where(row > col, 1.0, 0.0).astype(BF16)
    before = _dot(tri, member.astype(BF16)) + carry[...]
    carry[...] = carry[...] + jnp.sum(member, axis=0, keepdims=True)
    ti = jnp.zeros((tm, LANES), I32)
    tg = jnp.zeros((tm, LANES), F32)
    rk = jnp.zeros((tm, LANES), I32)
    for k in range(TOP_K):
        rank = jnp.sum(jnp.where(hots[k], before, 0.0), axis=1, keepdims=True).astype(I32)
        ti = jnp.where(lane == k, idxs[k], ti)
        tg = jnp.where(lane == k, es[k] / den, tg)
        rk = jnp.where(lane == k, rank, rk)
    ti_ref[...] = ti
    tg_ref[...] = tg
    rk_ref[...] = rk
    cnt_ref[...] = jnp.broadcast_to(carry[...], cnt_ref.shape)


def moe_router(x2, g, scale, shift, rw, rb, seq, tm=256):
    n_tok, d = x2.shape
    tm = min(tm, seq)
    rwp = jnp.zeros((d, LANES), F32).at[:, :N_EXPERTS].set(rw)
    rbp = jnp.full((1, LANES), NEG, F32).at[0, :N_EXPERTS].set(rb)
    row = lambda i: (i, 0)
    bsel = lambda i: (i * tm // seq, 0, 0)
    return pl.pallas_call(
        _router_kernel,
        out_shape=[jax.ShapeDtypeStruct((n_tok, d // 2), I32),
                   jax.ShapeDtypeStruct((n_tok, LANES), I32),
                   jax.ShapeDtypeStruct((n_tok, LANES), F32),
                   jax.ShapeDtypeStruct((n_tok, LANES), I32),
                   jax.ShapeDtypeStruct((8, LANES), F32)],
        grid=(n_tok // tm,),
        in_specs=[pl.BlockSpec((tm, d), row),
                  pl.BlockSpec((1, d), lambda i: (0, 0)),
                  pl.BlockSpec((1, 1, d), bsel),
                  pl.BlockSpec((1, 1, d), bsel),
                  pl.BlockSpec((d, LANES), lambda i: (0, 0)),
                  pl.BlockSpec((1, LANES), lambda i: (0, 0))],
        out_specs=[pl.BlockSpec((tm, d // 2), row),
                   pl.BlockSpec((tm, LANES), row),
                   pl.BlockSpec((tm, LANES), row),
                   pl.BlockSpec((tm, LANES), row),
                   pl.BlockSpec((8, LANES), lambda i: (0, 0))],
        scratch_shapes=[pltpu.VMEM((1, LANES), F32)],
        compiler_params=_cparams(("arbitrary",)),
        name="moe_router",
    )(x2, g.reshape(1, d), scale.reshape(-1, 1, d), shift.reshape(-1, 1, d), rwp, rbp)


def _dispatch_kernel(dest_ref, h_ref, xs_in_ref, xs_ref, sem):
    del xs_in_ref
    tm = h_ref.shape[0]

    def row_copy(i, d):
        return pltpu.make_async_copy(h_ref.at[pl.ds(i, 1)], xs_ref.at[pl.ds(d, 1)], sem)

    def issue(i, c):
        for k in range(TOP_K):
            row_copy(i, dest_ref[i * TOP_K + k]).start()
        return c

    lax.fori_loop(0, tm, issue, 0)

    def drain(i, c):
        for k in range(TOP_K):
            row_copy(0, 0).wait()
        return c

    lax.fori_loop(0, tm, drain, 0)


def moe_dispatch(h32, dest_flat, rows, tm=256):
    n_tok, dw = h32.shape
    tm = min(tm, n_tok)
    xs0 = jnp.zeros((rows, dw), I32)
    return pl.pallas_call(
        _dispatch_kernel,
        out_shape=jax.ShapeDtypeStruct((rows, dw), I32),
        grid=(n_tok // tm,),
        in_specs=[pl.BlockSpec((tm * TOP_K,), lambda i: (i,), memory_space=pltpu.SMEM),
                  pl.BlockSpec((tm, dw), lambda i: (i, 0)),
                  pl.BlockSpec(memory_space=pl.ANY)],
        out_specs=pl.BlockSpec(memory_space=pl.ANY),
        scratch_shapes=[pltpu.SemaphoreType.DMA(())],
        input_output_aliases={2: 0},
        compiler_params=_cparams(("arbitrary",)),
        name="moe_dispatch",
    )(dest_flat, h32, xs0)


def _experts_kernel(ce_ref, nu_ref, x_ref, w1_ref, b1_ref, w2_ref, b2_ref, y_ref, act_scr, *, tn):
    c = pl.program_id(0)
    de = act_scr.shape[1]

    @pl.when(c < nu_ref[0])
    def _():
        words = x_ref[...]
        x = jnp.concatenate([pltpu.bitcast(words << 16, F32),
                             pltpu.bitcast(words & jnp.int32(-65536), F32)], axis=1).astype(BF16)
        for j in range(de // tn):
            sl = slice(j * tn, (j + 1) * tn)
            sl2 = slice(de + j * tn, de + (j + 1) * tn)
            glu = _dot(x, w1_ref[0, :, sl]) + b1_ref[0, :, sl]
            lin = _dot(x, w1_ref[0, :, sl2]) + b1_ref[0, :, sl2]
            glu = jnp.minimum(glu, SWIGLU_LIMIT)
            lin = jnp.clip(lin, -SWIGLU_LIMIT, SWIGLU_LIMIT)
            act_scr[:, sl] = (glu * jax.nn.sigmoid(SWIGLU_ALPHA * glu) * (lin + 1.0)).astype(BF16)
        a = act_scr[...]
        d = y_ref.shape[1]
        for j in range(d // tn):
            sl = slice(j * tn, (j + 1) * tn)
            y_ref[:, sl] = _dot(a, w2_ref[0, :, sl]) + b2_ref[0, :, sl]

    @pl.when(c >= nu_ref[0])
    def _():
        y_ref[...] = jnp.zeros_like(y_ref)


def moe_experts(xs, chunk_exp, n_used, w1, b1, w2, b2, tn=256):
    rows = xs.shape[0]
    ne, d, de2 = w1.shape
    de = de2 // 2
    n_chunks = rows // MOE_CHUNK
    grid_spec = pltpu.PrefetchScalarGridSpec(
        num_scalar_prefetch=2,
        grid=(n_chunks,),
        in_specs=[pl.BlockSpec((MOE_CHUNK, d // 2), lambda c, ce, nu: (c, 0)),
                  pl.BlockSpec((1, d, de2), lambda c, ce, nu: (ce[c], 0, 0)),
                  pl.BlockSpec((1, 1, de2), lambda c, ce, nu: (ce[c], 0, 0)),
                  pl.BlockSpec((1, de, d), lambda c, ce, nu: (ce[c], 0, 0)),
                  pl.BlockSpec((1, 1, d), lambda c, ce, nu: (ce[c], 0, 0))],
        out_specs=pl.BlockSpec((MOE_CHUNK, d), lambda c, ce, nu: (c, 0)),
        scratch_shapes=[pltpu.VMEM((MOE_CHUNK, de), BF16)],
    )
    return pl.pallas_call(
        functools.partial(_experts_kernel, tn=min(tn, de)),
        out_shape=jax.ShapeDtypeStruct((rows, d), F32),
        grid_spec=grid_spec,
        compiler_params=_cparams(("arbitrary",), VMEM_LIMIT_BYTES),
        name="moe_experts",
    )(chunk_exp, n_used, xs, w1, b1.reshape(ne, 1, de2), w2, b2.reshape(ne, 1, d))


def _combine_kernel(dest_ref, y_ref, tg_ref, x_ref, gate_ref, o_ref, buf, sem):
    tm = x_ref.shape[0]

    def row_copy(i, k, d):
        return pltpu.make_async_copy(y_ref.at[pl.ds(d, 1)], buf.at[k, pl.ds(i, 1)], sem)

    def issue(i, c):
        for k in range(TOP_K):
            row_copy(i, k, dest_ref[i * TOP_K + k]).start()
        return c

    lax.fori_loop(0, tm, issue, 0)

    def drain(i, c):
        for k in range(TOP_K):
            row_copy(0, 0, 0).wait()
        return c

    lax.fori_loop(0, tm, drain, 0)
    tg = tg_ref[...]
    moe = tg[:, 0:1] * buf[0]
    for k in range(1, TOP_K):
        moe = moe + tg[:, k:k + 1] * buf[k]
    o_ref[...] = x_ref[...] + gate_ref[0] * moe


def moe_combine(y, dest_flat, tg, x2, gate, seq, tm=256):
    n_tok, d = x2.shape
    tm = min(tm, seq)
    row = lambda i: (i, 0)
    return pl.pallas_call(
        _combine_kernel,
        out_shape=jax.ShapeDtypeStruct((n_tok, d), F32),
        grid=(n_tok // tm,),
        in_specs=[pl.BlockSpec((tm * TOP_K,), lambda i: (i,), memory_space=pltpu.SMEM),
                  pl.BlockSpec(memory_space=pl.ANY),
                  pl.BlockSpec((tm, LANES), row),
                  pl.BlockSpec((tm, d), row),
                  pl.BlockSpec((1, 1, d), lambda i: (i * tm // seq, 0, 0))],
        out_specs=pl.BlockSpec((tm, d), row),
        scratch_shapes=[pltpu.VMEM((TOP_K, tm, d), F32), pltpu.SemaphoreType.DMA(())],
        compiler_params=_cparams(("arbitrary",), VMEM_LIMIT_BYTES),
        name="moe_combine",
    )(dest_flat, y, tg, x2, gate.reshape(-1, 1, d))


def moe_block(x2, g, scale, shift, gate, rw, rb, w1, b1, w2, b2, seq):
    n_tok, d = x2.shape
    h, ti, tg, rk, cnt = moe_router(x2, g, scale, shift, rw, rb, seq)
    counts = cnt[0, :N_EXPERTS].astype(I32)
    padded = ((counts + MOE_CHUNK - 1) // MOE_CHUNK) * MOE_CHUNK
    ends_pad = jnp.cumsum(padded)
    start_pad = ends_pad - padded
    n_chunks = -(-(n_tok * TOP_K) // MOE_CHUNK) + N_EXPERTS
    rows = n_chunks * MOE_CHUNK
    experts = jnp.arange(N_EXPERTS, dtype=I32)
    slot_start = jnp.sum(jnp.where(ti[:, :TOP_K, None] == experts, start_pad, 0), axis=-1)
    dest = (slot_start + rk[:, :TOP_K]).reshape(-1)
    chunk_pos = jnp.arange(n_chunks, dtype=I32)[:, None] * MOE_CHUNK
    chunk_exp = jnp.minimum(jnp.sum((ends_pad[None, :] <= chunk_pos).astype(I32), axis=1), N_EXPERTS - 1)
    n_used = (ends_pad[-1:] // MOE_CHUNK).astype(I32)
    xs = moe_dispatch(h, dest, rows)
    y = moe_experts(xs, chunk_exp, n_used, w1.astype(BF16), b1, w2.astype(BF16), b2)
    return moe_combine(y, dest, tg, x2, gate, seq)


def _dsa_index_kernel(qi_ref, k2_ref, wi_ref, mask_ref, ka_scr, kb_scr, key_scr,
                      *, S, KC, n_sel, idx_scale):
    qb = pl.program_id(1)
    R = qi_ref.shape[1]
    nchunks = S // KC
    lane = lax.broadcasted_iota(I32, (1, LANES), 1)

    @pl.when(qb == 0)
    def _():
        kk = k2_ref[0]
        ka_scr[...] = jnp.where(lane < HEAD_DIM, kk, 0.0).astype(BF16)
        kb_scr[...] = jnp.where(lane >= HEAD_DIM, kk, 0.0).astype(BF16)

    nck = (qb * R + R + KC - 1) // KC
    w = wi_ref[0].astype(BF16).astype(F32)
    qv = qi_ref[0].astype(BF16)

    def relu_bf16(d):
        return jnp.maximum(d, 0.0).astype(BF16).astype(F32)

    row = qb * R + lax.broadcasted_iota(I32, (R, KC), 0)
    col0 = lax.broadcasted_iota(I32, (R, KC), 1)

    def sort_key(x):
        bits = pltpu.bitcast(x, I32)
        return jnp.where(bits < 0, bits ^ jnp.int32(0x7FFFFFFF), bits)

    def score_chunk(c, gmax):
        off = pl.multiple_of(c * KC, KC)
        ka = ka_scr[pl.ds(off, KC), :]
        kb = kb_scr[pl.ds(off, KC), :]
        sc = jnp.zeros((R, KC), F32)
        for pp in range(N_IDX_HEADS // 2):
            qp = qv[:, pp * LANES:(pp + 1) * LANES]
            sc = sc + w[:, 2 * pp:2 * pp + 1] * relu_bf16(_dot_nt(qp, ka))
            sc = sc + w[:, 2 * pp + 1:2 * pp + 2] * relu_bf16(_dot_nt(qp, kb))
        sc = sc * idx_scale
        sc = jnp.where(sc == 0.0, 0.0, sc)
        sc = jnp.where(off + col0 <= row, sc, -jnp.inf)
        key_scr[c] = jnp.where(sc == -jnp.inf, INT_MIN, sort_key(sc))
        g0, g1 = gmax
        for j in range(0, KC // LANES, 2):
            g0 = jnp.maximum(g0, sc[:, j * LANES:(j + 1) * LANES])
            g1 = jnp.maximum(g1, sc[:, (j + 1) * LANES:(j + 2) * LANES])
        return g0, g1

    neg_inf = jnp.full((R, LANES), -jnp.inf, F32)
    g0, g1 = lax.fori_loop(0, nck, score_chunk, (neg_inf, neg_inf))
    assert n_sel <= 2 * LANES
    lo_f = jnp.min(jnp.minimum(g0, g1), axis=1, keepdims=True)
    hi_f = jnp.max(jnp.maximum(g0, g1), axis=1, keepdims=True)
    lo_key = jnp.broadcast_to(jnp.where(lo_f == -jnp.inf, INT_MIN, sort_key(lo_f)), (R, LANES))
    hi_key = jnp.broadcast_to(jnp.where(hi_f == -jnp.inf, INT_MIN, sort_key(hi_f)), (R, LANES))
    shared = jnp.min(lax.clz(lo_key ^ hi_key).astype(F32)).astype(I32)
    prefix_mask = jnp.where(shared == 0, 0, jnp.int32(-1) << (32 - jnp.maximum(shared, 1)))
    start = jnp.where(shared == 0, jnp.int32(INT_MIN), lo_key & prefix_mask)

    def count(pred):
        def body(c, acc):
            kc = key_scr[c]
            for j in range(KC // LANES):
                acc = acc + jnp.where(pred(kc[:, j * LANES:(j + 1) * LANES], c * KC + j * LANES), 1.0, 0.0)
            return acc
        acc = lax.fori_loop(0, nck, body, jnp.zeros((R, LANES), F32))
        return jnp.broadcast_to(jnp.sum(acc, axis=1, keepdims=True), (R, LANES))

    nsel_f = float(n_sel)

    def bit_step(t, cur, cnt_cur):
        cand = cur + (jnp.int32(1) << jnp.maximum(31 - t, 0))
        cnt = count(lambda kc, base: kc >= cand)
        ok = jnp.logical_and(cnt >= nsel_f, t < 32)
        return jnp.where(ok, cand, cur), jnp.where(ok, cnt, cnt_cur)

    def two_bits(carry):
        t, cur, cnt_cur, _ = carry
        cur, cnt_cur = bit_step(t, cur, cnt_cur)
        cur, cnt_cur = bit_step(t + 1, cur, cnt_cur)
        settled = (jnp.min(jnp.where(cnt_cur == nsel_f, 1.0, 0.0)) > 0.0).astype(I32)
        return t + 2, cur, cnt_cur, settled

    _, tau, cnt_tau, _ = lax.while_loop(
        lambda carry: jnp.logical_and(carry[0] < 32, carry[3] == 0),
        two_bits,
        (shared, start, jnp.full((R, LANES), float(S), F32), jnp.int32(0)))

    overflow = jnp.logical_and(tau > INT_MIN, cnt_tau > nsel_f)
    any_over = jnp.max(jnp.where(overflow, 1.0, 0.0)) > 0.0
    lane_r = lax.broadcasted_iota(I32, (R, LANES), 1)

    def write_mask(keep_fn):
        for c in range(nchunks):
            @pl.when(c < nck)
            def _():
                kc = key_scr[c]
                for j in range(KC // LANES):
                    keep = keep_fn(kc[:, j * LANES:(j + 1) * LANES], c * KC + j * LANES + lane_r)
                    mask_ref[0, :, c * KC + j * LANES:c * KC + (j + 1) * LANES] = (
                        jnp.where(keep, 1, 0).astype(jnp.int8))

            @pl.when(c >= nck)
            def _():
                mask_ref[0, :, c * KC:(c + 1) * KC] = jnp.zeros((R, KC), jnp.int8)

    @pl.when(jnp.logical_not(any_over))
    def _():
        floor = jnp.maximum(tau, INT_MIN + 1)
        write_mask(lambda kj, pos: kj >= floor)

    @pl.when(any_over)
    def _():
        need = nsel_f - count(lambda kc, base: kc > tau)

        def jstep(t, cur):
            cand = cur + (jnp.int32(1) << (S.bit_length() - 2 - t))
            cnt = count(lambda kc, base: jnp.logical_and(kc == tau, base + lane_r < cand))
            return jnp.where(cnt < need, cand, cur)

        jlim = lax.fori_loop(0, S.bit_length() - 1, jstep, jnp.zeros((R, LANES), I32))
        write_mask(lambda kj, pos: jnp.logical_and(
            jnp.logical_or(kj > tau, jnp.logical_and(kj == tau, pos <= jlim)), kj > INT_MIN))


def dsa_index_mask(aux, qcol, kcol, wcol, n_sel, R=128, KC=512):
    bsz, seq, _ = aux.shape
    R = min(R, seq)
    KC = min(KC, seq)
    nq = N_IDX_HEADS * HEAD_DIM
    return pl.pallas_call(
        functools.partial(_dsa_index_kernel, S=seq, KC=KC, n_sel=n_sel,
                          idx_scale=float(nq ** -0.5)),
        out_shape=jax.ShapeDtypeStruct((bsz, seq, seq), jnp.int8),
        grid=(bsz, seq // R),
        in_specs=[pl.BlockSpec((1, R, nq), lambda b, q: (b, q, qcol // nq)),
                  pl.BlockSpec((1, seq, LANES), lambda b, q: (b, 0, kcol // LANES)),
                  pl.BlockSpec((1, R, LANES), lambda b, q: (b, q, wcol // LANES))],
        out_specs=pl.BlockSpec((1, R, seq), lambda b, q: (b, q, 0)),
        scratch_shapes=[pltpu.VMEM((seq, LANES), BF16), pltpu.VMEM((seq, LANES), BF16),
                        pltpu.VMEM((seq // KC, R, KC), I32)],
        compiler_params=_cparams(("arbitrary", "arbitrary"), VMEM_LIMIT_BYTES),
        name="dsa_index_mask",
    )(aux, aux, aux)


def _dsa_attn_kernel(qi_ref, ki_ref, q_ref, k_ref, v_ref, mask_ref, tz_ref, o_ref,
                     m_scr, l_scr, acc_scr, *, TQ, TK, DCL):
    s = pl.program_id(1)
    qi = qi_ref[s]
    ki = ki_ref[s]
    k_last = ((qi + 1) * TQ - 1) // TK

    @pl.when(ki == 0)
    def _():
        m_scr[...] = jnp.full_like(m_scr, NEG)
        l_scr[...] = jnp.zeros_like(l_scr)
        acc_scr[...] = jnp.zeros_like(acc_scr)

    lane = lax.broadcasted_iota(I32, (1, LANES), 1)
    lo_half = lane < HEAD_DIM
    d0 = qi * TQ - ki * TK
    cb0 = (DCL - jnp.minimum(d0, DCL)) // LANES
    madd = jnp.where(mask_ref[0].astype(I32) != 0, 0.0, NEG).astype(BF16)
    npair = q_ref.shape[2] // LANES

    def logits(h):
        cs = slice((h // 2) * LANES, (h // 2 + 1) * LANES)
        qsel = lo_half if h % 2 == 0 else jnp.logical_not(lo_half)
        q = _keep_lanes(q_ref[0, :, cs] * QK_SCALE, qsel)
        bias = jnp.concatenate([tz_ref[h, cb0 + jj] for jj in range(TK // LANES)], axis=1)
        return _dot_nt(q, k_ref[0, :, cs]).astype(BF16) + bias + madd

    def weighted_values(h, pm):
        return _dot(pm, v_ref[0, :, h * LANES:(h + 1) * LANES])

    def accumulate(pp, alphas, pvs):
        cs = slice(pp * LANES, (pp + 1) * LANES)
        acc_scr[:, cs] = (acc_scr[:, cs] * jnp.where(lo_half, alphas[0], alphas[1])
                          + jnp.where(lo_half, pvs[0], pvs[1]))
        l_scr[pp] = (l_scr[pp] * jnp.where(lo_half, alphas[1], alphas[0])
                     + jnp.where(lo_half, pvs[1], pvs[0]))

    ahead = 2
    nh = 2 * npair
    queue = [logits(h) for h in range(ahead)]
    alphas, pvs, probs = {}, {}, {}
    for h in range(nh + 1):
        if h < nh:
            sc = queue.pop(0)
            if h + ahead < nh:
                queue.append(logits(h + ahead))
            m_scr[h], alphas[h], probs[h] = _online_softmax(sc, m_scr[h])
        if h >= 1:
            pvs[h - 1] = weighted_values(h - 1, probs.pop(h - 1))
            if (h - 1) % 2 == 1:
                accumulate((h - 1) // 2, (alphas[h - 2], alphas[h - 1]), (pvs[h - 2], pvs[h - 1]))

    @pl.when(ki == k_last)
    def _():
        for pp in range(npair):
            cs = slice(pp * LANES, (pp + 1) * LANES)
            l = pltpu.roll(l_scr[pp], HEAD_DIM, axis=1)
            o_ref[0, :, cs] = (acc_scr[:, cs] / l).astype(o_ref.dtype)


def _dsa_bias_table(rel_bias, seq, TQ, TK, DCL):
    i = np.arange(TQ)[:, None]
    c = np.arange(DCL + TK)[None, :]
    d = np.clip(i - c + DCL, 0, seq - 1)
    tz = _bucket_bias(jnp.asarray(d, I32), rel_bias)
    tz = tz.reshape(rel_bias.shape[1], TQ, (DCL + TK) // LANES, LANES)
    return tz.transpose(0, 2, 1, 3).astype(BF16)


def _np_t5_bucket(d):
    max_exact = N_BUCKETS // 2
    df = np.maximum(d, 1).astype(np.float32)
    large = max_exact + (np.log(df / max_exact) / math.log(MAX_DISTANCE / max_exact)
                         * (N_BUCKETS - max_exact)).astype(np.int32)
    return np.where(d < max_exact, d, np.minimum(large, N_BUCKETS - 1))


def dsa_attention(q_src, qcol, kv, mask, rel_bias, TQ=256, TK=1024):
    bsz, seq, _ = q_src.shape
    TQ = min(TQ, seq)
    TK = min(TK, seq)
    hw = N_DSA_HEADS * HEAD_DIM
    buckets = _np_t5_bucket(np.arange(seq))
    first_const = int(np.max(np.nonzero(buckets != buckets[-1])[0])) + 1
    DCL = min(seq, -(-(first_const + TK - 1) // LANES) * LANES)
    tz = _dsa_bias_table(rel_bias, seq, TQ, TK, DCL)
    qs, ks = _tri_schedule(seq // TQ, TK // TQ)
    grid_spec = pltpu.PrefetchScalarGridSpec(
        num_scalar_prefetch=2,
        grid=(bsz, qs.shape[0]),
        in_specs=[
            pl.BlockSpec((1, TQ, hw), lambda b, s, qi, ki: (b, qi[s], qcol // hw)),
            pl.BlockSpec((1, TK, hw), lambda b, s, qi, ki: (b, ki[s], 0)),
            pl.BlockSpec((1, TK, 2 * hw), lambda b, s, qi, ki: (b, ki[s], 0)),
            pl.BlockSpec((1, TQ, TK), lambda b, s, qi, ki: (b, qi[s], ki[s])),
            pl.BlockSpec(tz.shape, lambda b, s, qi, ki: (0, 0, 0, 0), pipeline_mode=pl.Buffered(1)),
        ],
        out_specs=pl.BlockSpec((1, TQ, hw), lambda b, s, qi, ki: (b, qi[s], 0)),
        scratch_shapes=[pltpu.VMEM((N_DSA_HEADS, TQ, LANES), F32),
                        pltpu.VMEM((N_DSA_HEADS // 2, TQ, LANES), F32),
                        pltpu.VMEM((TQ, hw), F32)],
    )
    return pl.pallas_call(
        functools.partial(_dsa_attn_kernel, TQ=TQ, TK=TK, DCL=DCL),
        out_shape=jax.ShapeDtypeStruct((bsz, seq, hw), BF16),
        grid_spec=grid_spec,
        compiler_params=_cparams(("arbitrary", "arbitrary"), VMEM_LIMIT_BYTES),
        name="dsa_attention",
    )(qs, ks, q_src, kv, _values_with_ones(kv[:, :, hw:], N_DSA_HEADS), mask, tz)


def _rms_kernel(x_ref, g_ref, o_ref):
    x = x_ref[...]
    ms = jnp.mean(x * x, axis=-1, keepdims=True)
    o_ref[...] = x * lax.rsqrt(ms + EPS) * g_ref[...]


def final_rmsnorm(x2, g, tm=512):
    n_tok, d = x2.shape
    tm = min(tm, n_tok)
    return pl.pallas_call(
        _rms_kernel,
        out_shape=jax.ShapeDtypeStruct((n_tok, d), F32),
        grid=(n_tok // tm,),
        in_specs=[pl.BlockSpec((tm, d), lambda i: (i, 0)), pl.BlockSpec((1, d), lambda i: (0, 0))],
        out_specs=pl.BlockSpec((tm, d), lambda i: (i, 0)),
        compiler_params=_cparams(("arbitrary",)),
        name="final_rmsnorm",
    )(x2, g.reshape(1, d))


def even_mixer_residual(x2, shift1, scale1, gate1, norm1, w_in, fox_fb, w_out, rel_bias, bsz, seq):
    d = x2.shape[1]
    nfq = 3 * N_FOX_HEADS * HEAD_DIM
    w_main = jnp.concatenate([w_in[:, :nfq], w_in[:, nfq + N_FOX_HEADS:]], axis=1).astype(BF16)
    rep = np.repeat(np.arange(N_FOX_HEADS), FOX_PARTS)
    gcols = np.concatenate([rep, rep])
    w_gate = jnp.zeros((d, LANES), F32).at[:, :2 * FOX_XW].set(w_in[:, nfq + gcols])
    fb = jnp.zeros((1, LANES), F32).at[0, :2 * FOX_XW].set(fox_fb[gcols])
    proj, gate_z = norm_mod_matmul(x2, norm1, scale1, shift1, w_main, w_gate, seq)
    c = proj.shape[1]
    proj3 = proj.reshape(bsz, seq, c)
    qx, kx = fox_gate_columns(gate_z.reshape(bsz, seq, LANES), fb)
    fox = fox_attention(proj3, qx, kx, N_FOX_HEADS)
    dil_o, dil_l = [], []
    for g, (window, dil) in enumerate(DIL_PAIRS):
        bias = _dil_bias_tiles(rel_bias, g, window, dil)
        o, l = dilated_group_attention(proj3, bias, g, dil, nfq + g * N_DIL_SLOTS * HEAD_DIM)
        dil_o.append(o)
        dil_l.append(l)
    return out_proj_even(fox.reshape(bsz * seq, -1), dil_o, dil_l, w_out.astype(BF16), x2, gate1, seq)


def odd_mixer_residual(x2, shift1, scale1, gate1, norm1, w_in, kv_norm, w_ukv, w_out, rel_bias, bsz, seq):
    d = x2.shape[1]
    nq = N_DSA_HEADS * HEAD_DIM
    ni = N_IDX_HEADS * HEAD_DIM
    c_ckv, c_qi = nq, nq + KV_RANK
    c_ki = c_qi + ni
    c_wi = c_ki + HEAD_DIM
    w_main = w_in[:, :nq].astype(BF16)
    c_w, c_k, c_q = KV_RANK, KV_RANK + LANES, KV_RANK + 2 * LANES
    w_aux = jnp.zeros((d, c_q + ni), F32)
    w_aux = w_aux.at[:, :KV_RANK].set(w_in[:, c_ckv:c_qi])
    w_aux = w_aux.at[:, c_w:c_w + N_IDX_HEADS].set(w_in[:, c_wi:])
    w_aux = w_aux.at[:, c_k:c_k + HEAD_DIM].set(w_in[:, c_ki:c_wi])
    w_aux = w_aux.at[:, c_k + HEAD_DIM:c_q].set(w_in[:, c_ki:c_wi])
    w_aux = w_aux.at[:, c_q:].set(w_in[:, c_qi:c_ki])
    proj, aux = norm_mod_matmul(x2, norm1, scale1, shift1, w_main, w_aux, seq)
    zeros = jnp.zeros((bsz, KV_RANK), F32)
    kv = norm_mod_matmul(aux[:, :KV_RANK], kv_norm, zeros, zeros, w_ukv.astype(BF16), None, seq)
    proj3 = proj.reshape(bsz, seq, -1)
    n_sel = min(DSA_TOPK, seq // 4)
    mask = dsa_index_mask(aux.reshape(bsz, seq, -1), c_q, c_k, c_w, n_sel)
    att = dsa_attention(proj3, 0, kv.reshape(bsz, seq, -1), mask, rel_bias)
    return out_proj_odd(att.reshape(bsz * seq, -1), w_out.astype(BF16), x2, gate1, seq)


def kernel(x, c, rel_bias, l0_norm1, l0_ada_w, l0_ada_b, l0_w_in, l0_fox_fb, l0_w_out, l0_norm2, l0_router_w, l0_router_b, l0_w1, l0_b1, l0_w2, l0_b2, l1_norm1, l1_ada_w, l1_ada_b, l1_w_in, l1_kv_norm, l1_w_ukv, l1_w_out, l1_norm2, l1_router_w, l1_router_b, l1_w1, l1_b1, l1_w2, l1_b2, final_norm):
    bsz, seq, d = x.shape
    x2 = x.reshape(bsz * seq, d)
    layers = (
        (l0_norm1, l0_ada_w, l0_ada_b, l0_norm2, l0_router_w, l0_router_b, l0_w1, l0_b1, l0_w2, l0_b2),
        (l1_norm1, l1_ada_w, l1_ada_b, l1_norm2, l1_router_w, l1_router_b, l1_w1, l1_b1, l1_w2, l1_b2),
    )
    for i, (norm1, ada_w, ada_b, norm2, rw, rb, w1, b1, w2, b2) in enumerate(layers):
        mods = ada_mods(c, ada_w, ada_b)
        shift1, scale1, gate1, shift2, scale2, gate2 = (mods[:, j * d:(j + 1) * d] for j in range(6))
        if i % 2 == 0:
            x2 = even_mixer_residual(x2, shift1, scale1, gate1, norm1, l0_w_in, l0_fox_fb, l0_w_out,
                                     rel_bias, bsz, seq)
        else:
            x2 = odd_mixer_residual(x2, shift1, scale1, gate1, norm1, l1_w_in, l1_kv_norm, l1_w_ukv,
                                    l1_w_out, rel_bias, bsz, seq)
        x2 = moe_block(x2, norm2, scale2, shift2, gate2, rw, rb, w1, b1, w2, b2, seq)
    return final_rmsnorm(x2, final_norm).reshape(bsz, seq, d)
```

```python
import functools
import math

import numpy as np
import jax
import jax.numpy as jnp
from jax import lax
from jax.experimental import pallas as pl
from jax.experimental.pallas import tpu as pltpu

F32, BF16, I32 = jnp.float32, jnp.bfloat16, jnp.int32

LANES = 128
VMEM_LIMIT_BYTES = 56 * 1024 * 1024

HEAD_DIM = 64
N_FOX_HEADS = 8
DIL_PAIRS = ((128, 1), (512, 4), (2048, 16))
N_DIL_SLOTS = 4
N_DSA_HEADS = 16
KV_RANK = 256
N_IDX_HEADS = 8
DSA_TOPK = 256
N_BUCKETS = 32
MAX_DISTANCE = 2048
N_EXPERTS = 32
TOP_K = 4
D_EXPERT = 1024
SWIGLU_ALPHA = 1.702
SWIGLU_LIMIT = 7.0
MOE_CHUNK = 256
EPS = 1e-6

NEG = -1e30
INT_MIN = -(2 ** 31)
QK_SCALE = HEAD_DIM ** -0.5
FOX_PARTS = 3
FOX_XW = FOX_PARTS * N_FOX_HEADS


def _cparams(sem, vmem=None):
    return pltpu.CompilerParams(dimension_semantics=sem, vmem_limit_bytes=vmem)


def _dot_nt(a, b):
    return lax.dot_general(a, b, (((1,), (1,)), ((), ())), preferred_element_type=F32)


def _dot(a, b):
    return jnp.dot(a, b, preferred_element_type=F32)


def _dot_f32(a, b):
    return jnp.dot(a, b, precision=lax.Precision.HIGHEST, preferred_element_type=F32)


def _keep_lanes(x, sel):
    return jnp.where(sel, x.astype(F32), 0.0).astype(x.dtype)


def _online_softmax(sc, m_prev):
    cols = [sc[:, j:j + LANES] for j in range(0, sc.shape[1], LANES)]
    m_cur = functools.reduce(jnp.maximum, cols).astype(F32)
    m_new = jnp.maximum(m_prev, jnp.max(m_cur, axis=1, keepdims=True))
    alpha = jnp.exp(m_prev - m_new)
    shift = m_new.astype(BF16)
    return m_new, alpha, jnp.concatenate([jnp.exp(c - shift) for c in cols], axis=1)


def _values_with_ones(v, n_heads):
    lead = v.shape[:-1]
    v4 = v.reshape(*lead, n_heads // 2, 2, HEAD_DIM)
    ones = jnp.ones((*lead, n_heads // 2, HEAD_DIM), v.dtype)
    va = jnp.concatenate([v4[..., 0, :], ones], axis=-1)
    vb = jnp.concatenate([ones, v4[..., 1, :]], axis=-1)
    return jnp.stack([va, vb], axis=-2).reshape(*lead, n_heads * 2 * HEAD_DIM)


def _tri_schedule(nq, kq_ratio=1):
    qs, ks = [], []
    for qi in range(nq):
        for ki in range(qi // kq_ratio + 1):
            qs.append(qi)
            ks.append(ki)
    return jnp.asarray(qs, I32), jnp.asarray(ks, I32)


def _ada_kernel(c_ref, w_ref, b_ref, o_ref):
    c = c_ref[...]
    sc = c * jax.nn.sigmoid(c)
    o_ref[...] = _dot_f32(sc, w_ref[...]) + b_ref[...]


def ada_mods(c, w, b):
    bsz, d = c.shape
    n = w.shape[1]
    tn = min(n, 1024)
    cp = jnp.zeros((8, d), F32).at[:bsz].set(c)
    out = pl.pallas_call(
        _ada_kernel,
        out_shape=jax.ShapeDtypeStruct((8, n), F32),
        grid=(n // tn,),
        in_specs=[pl.BlockSpec((8, d), lambda j: (0, 0)),
                  pl.BlockSpec((d, tn), lambda j: (0, j)),
                  pl.BlockSpec((1, tn), lambda j: (0, j))],
        out_specs=pl.BlockSpec((8, tn), lambda j: (0, j)),
        compiler_params=_cparams(("arbitrary",)),
        name="ada_mods",
    )(cp, w, b.reshape(1, n))
    return out[:bsz]


def _nmm_kernel(x_ref, g_ref, sc_ref, sh_ref, w_ref, *rest, tn, has_aux):
    if has_aux:
        wa_ref, o_ref, oa_ref = rest
    else:
        (o_ref,) = rest
    x = x_ref[...]
    ms = jnp.mean(x * x, axis=-1, keepdims=True)
    y = x * lax.rsqrt(ms + EPS) * g_ref[...]
    hf = y * (1.0 + sc_ref[0]) + sh_ref[0]
    h = hf.astype(BF16)
    n = o_ref.shape[1]
    for j0 in range(0, n, tn):
        j1 = min(j0 + tn, n)
        o_ref[:, j0:j1] = _dot(h, w_ref[:, j0:j1]).astype(o_ref.dtype)
    if has_aux:
        na = oa_ref.shape[1]
        for j0 in range(0, na, tn):
            j1 = min(j0 + tn, na)
            oa_ref[:, j0:j1] = _dot_f32(hf, wa_ref[:, j0:j1])


def norm_mod_matmul(x2, g, scale, shift, w, w_aux, seq, tm=512, tn=256):
    n_tok, din = x2.shape
    nout = w.shape[1]
    tm = min(tm, seq)
    has_aux = w_aux is not None
    bsel = lambda i: (i * tm // seq, 0, 0)
    in_specs = [pl.BlockSpec((tm, din), lambda i: (i, 0)),
                pl.BlockSpec((1, din), lambda i: (0, 0)),
                pl.BlockSpec((1, 1, din), bsel),
                pl.BlockSpec((1, 1, din), bsel),
                pl.BlockSpec((din, nout), lambda i: (0, 0))]
    out_shape = [jax.ShapeDtypeStruct((n_tok, nout), BF16)]
    out_specs = [pl.BlockSpec((tm, nout), lambda i: (i, 0))]
    args = [x2, g.reshape(1, din), scale.reshape(-1, 1, din), shift.reshape(-1, 1, din), w]
    if has_aux:
        na = w_aux.shape[1]
        in_specs.append(pl.BlockSpec((din, na), lambda i: (0, 0)))
        out_shape.append(jax.ShapeDtypeStruct((n_tok, na), F32))
        out_specs.append(pl.BlockSpec((tm, na), lambda i: (i, 0)))
        args.append(w_aux)
    outs = pl.pallas_call(
        functools.partial(_nmm_kernel, tn=min(tn, nout), has_aux=has_aux),
        out_shape=out_shape,
        grid=(n_tok // tm,),
        in_specs=in_specs,
        out_specs=out_specs,
        compiler_params=_cparams(("arbitrary",), VMEM_LIMIT_BYTES),
        name="norm_mod_matmul",
    )(*args)
    return outs if has_aux else outs[0]


def _foxcum_kernel(z_ref, fb_ref, qx_ref, kx_ref, carry):
    t = pl.program_id(1)

    @pl.when(t == 0)
    def _():
        carry[...] = jnp.zeros_like(carry)

    ts = z_ref.shape[1]
    z = z_ref[0] + fb_ref[...]
    lf = jnp.minimum(z, 0.0) - jnp.log1p(jnp.exp(-jnp.abs(z)))
    row = lax.broadcasted_iota(I32, (ts, ts), 0)
    col = lax.broadcasted_iota(I32, (ts, ts), 1)
    tri = jnp.where(row >= col, 1.0, 0.0).astype(F32)
    c = jnp.dot(tri, lf, precision=lax.Precision.HIGHEST, preferred_element_type=F32) + carry[...]
    carry[...] = c[ts - 1:ts, :]
    hi = c.astype(BF16).astype(F32)
    r1 = c - hi
    mid = r1.astype(BF16).astype(F32)
    lo = (r1 - mid).astype(BF16).astype(F32)
    lane = lax.broadcasted_iota(I32, (ts, LANES), 1)
    ph = lane % FOX_PARTS
    part = jnp.where(ph == 0, hi, jnp.where(ph == 1, mid, lo))
    first = lane < FOX_XW
    second = (lane >= FOX_XW) & (lane < 2 * FOX_XW)
    qx_ref[0] = jnp.where(first, part, jnp.where(second, 1.0, 0.0)).astype(BF16)
    kx_ref[0] = jnp.where(first, 1.0, jnp.where(second, -part, 0.0)).astype(BF16)


def fox_gate_columns(z, fb, ts=512):
    bsz, seq, _ = z.shape
    ts = min(ts, seq)
    return pl.pallas_call(
        _foxcum_kernel,
        out_shape=[jax.ShapeDtypeStruct((bsz, seq, LANES), BF16)] * 2,
        grid=(bsz, seq // ts),
        in_specs=[pl.BlockSpec((1, ts, LANES), lambda b, t: (b, t, 0)),
                  pl.BlockSpec((1, LANES), lambda b, t: (0, 0))],
        out_specs=[pl.BlockSpec((1, ts, LANES), lambda b, t: (b, t, 0))] * 2,
        scratch_shapes=[pltpu.VMEM((1, LANES), F32)],
        compiler_params=_cparams(("arbitrary", "arbitrary")),
        name="fox_gate_columns",
    )(z, fb)


def _fox_kernel(qi_ref, ki_ref, q_ref, k_ref, v_ref, qx_ref, kx_ref, o_ref, m_scr, l_scr, acc_scr, *, T):
    p = pl.program_id(1)
    s = pl.program_id(2)
    qi = qi_ref[s]
    ki = ki_ref[s]

    @pl.when(ki == 0)
    def _():
        m_scr[...] = jnp.full_like(m_scr, NEG)
        l_scr[...] = jnp.zeros_like(l_scr)
        acc_scr[...] = jnp.zeros_like(acc_scr)

    lane = lax.broadcasted_iota(I32, (1, LANES), 1)
    lo_half = lane < HEAD_DIM

    def step(masked):
        q = q_ref[0] * QK_SCALE
        qx = qx_ref[0]
        kcat = jnp.concatenate([k_ref[0], kx_ref[0]], axis=1)
        if masked:
            row = lax.broadcasted_iota(I32, (T, T), 0)
            col = lax.broadcasted_iota(I32, (T, T), 1)
            causal_add = jnp.where(col <= row, 0.0, NEG).astype(BF16)
        scs = []
        for hh in range(2):
            xl = FOX_PARTS * (2 * p + hh)
            qsel = lo_half if hh == 0 else jnp.logical_not(lo_half)
            xsel = ((lane >= xl) & (lane < xl + FOX_PARTS)) | (
                (lane >= FOX_XW + xl) & (lane < FOX_XW + xl + FOX_PARTS))
            qcat = jnp.concatenate([_keep_lanes(q, qsel), _keep_lanes(qx, xsel)], axis=1)
            sc = _dot_nt(qcat, kcat).astype(BF16)
            scs.append(sc + causal_add if masked else sc)
        pvs, alphas = [], []
        for hh in range(2):
            m_scr[hh], alpha, pm = _online_softmax(scs[hh], m_scr[hh])
            pvs.append(_dot(pm, v_ref[0, :, hh * LANES:(hh + 1) * LANES]))
            alphas.append(alpha)
        acc_scr[...] = (acc_scr[...] * jnp.where(lo_half, alphas[0], alphas[1])
                        + jnp.where(lo_half, pvs[0], pvs[1]))
        l_scr[...] = (l_scr[...] * jnp.where(lo_half, alphas[1], alphas[0])
                      + jnp.where(lo_half, pvs[1], pvs[0]))

    @pl.when(ki < qi)
    def _():
        step(False)

    @pl.when(ki == qi)
    def _():
        step(True)
        l = pltpu.roll(l_scr[...], HEAD_DIM, axis=1)
        o_ref[0] = (acc_scr[...] / l).astype(o_ref.dtype)


def fox_attention(proj, qx, kx, n_heads, T=1024):
    bsz, seq, _ = proj.shape
    T = min(T, seq)
    npair = n_heads // 2
    hw = n_heads * HEAD_DIM
    v_ones = _values_with_ones(proj[:, :, 2 * hw:3 * hw], n_heads)
    qs, ks = _tri_schedule(seq // T)
    grid_spec = pltpu.PrefetchScalarGridSpec(
        num_scalar_prefetch=2,
        grid=(bsz, npair, qs.shape[0]),
        in_specs=[
            pl.BlockSpec((1, T, LANES), lambda b, p, s, qi, ki: (b, qi[s], p)),
            pl.BlockSpec((1, T, LANES), lambda b, p, s, qi, ki: (b, ki[s], npair + p)),
            pl.BlockSpec((1, T, 2 * LANES), lambda b, p, s, qi, ki: (b, ki[s], p)),
            pl.BlockSpec((1, T, LANES), lambda b, p, s, qi, ki: (b, qi[s], 0)),
            pl.BlockSpec((1, T, LANES), lambda b, p, s, qi, ki: (b, ki[s], 0)),
        ],
        out_specs=pl.BlockSpec((1, T, LANES), lambda b, p, s, qi, ki: (b, qi[s], p)),
        scratch_shapes=[pltpu.VMEM((2, T, LANES), F32), pltpu.VMEM((T, LANES), F32),
                        pltpu.VMEM((T, LANES), F32)],
    )
    return pl.pallas_call(
        functools.partial(_fox_kernel, T=T),
        out_shape=jax.ShapeDtypeStruct((bsz, seq, npair * LANES), BF16),
        grid_spec=grid_spec,
        compiler_params=_cparams(("arbitrary", "arbitrary", "arbitrary"), VMEM_LIMIT_BYTES),
        name="fox_attention",
    )(qs, ks, proj, proj, v_ones, qx, kx)


def _dil_kernel(q_ref, kp_ref, kc_ref, vp_ref, vc_ref, bias_ref, o_ref, lse_ref, *, W, M):
    ut = pl.program_id(2)
    lane = lax.broadcasted_iota(I32, (1, LANES), 1)
    lo_half = lane < HEAD_DIM
    col = lax.broadcasted_iota(I32, (W, 2 * W), 1)
    no_prev = jnp.logical_and(ut == 0, col < W)
    for i in range(M):
        rows = slice(i * W, (i + 1) * W)
        for pp in range(2):
            cs = slice(pp * LANES, (pp + 1) * LANES)
            q = q_ref[0, rows, cs] * QK_SCALE
            if i == 0:
                k = jnp.concatenate([kp_ref[0, :, cs], kc_ref[0, rows, cs]], axis=0)
                v = jnp.concatenate([vp_ref[0, :, cs], vc_ref[0, rows, cs]], axis=0)
            else:
                k = kc_ref[0, (i - 1) * W:(i + 1) * W, cs]
                v = vc_ref[0, (i - 1) * W:(i + 1) * W, cs]
            outs, lses = [], []
            for hh in range(2):
                qsel = lo_half if hh == 0 else jnp.logical_not(lo_half)
                sc = _dot_nt(_keep_lanes(q, qsel), k) + bias_ref[pp, hh]
                if i == 0:
                    sc = jnp.where(no_prev, NEG, sc)
                mx = jnp.max(sc, axis=1, keepdims=True)
                e = jnp.exp(sc - mx)
                den = jnp.sum(e, axis=1, keepdims=True)
                outs.append(_dot((e / den).astype(BF16), v))
                lses.append(mx + jnp.log(den))
            o_ref[0, rows, cs] = jnp.where(lo_half, outs[0], outs[1])
            lse_ref[0, rows, cs] = jnp.where(lo_half, lses[0], lses[1])


def dilated_group_attention(proj, bias, g, dil, col0, W=128):
    bsz, seq, c = proj.shape
    sub = seq // dil
    pv = proj.reshape(bsz, sub, dil * c)
    gw = N_DIL_SLOTS * HEAD_DIM
    M = min(4, sub // W)
    cb = c // gw
    qb = col0 // gw
    kb = qb + 3
    vb = kb + 3
    assert c % gw == 0 and col0 % gw == 0
    prev = lambda u: jnp.maximum(u * M - 1, 0)
    out = pl.pallas_call(
        functools.partial(_dil_kernel, W=W, M=M),
        out_shape=[jax.ShapeDtypeStruct((bsz, sub, dil * gw), F32)] * 2,
        grid=(bsz, dil, sub // (M * W)),
        in_specs=[
            pl.BlockSpec((1, M * W, gw), lambda b, a, u: (b, u, a * cb + qb)),
            pl.BlockSpec((1, W, gw), lambda b, a, u: (b, prev(u), a * cb + kb)),
            pl.BlockSpec((1, M * W, gw), lambda b, a, u: (b, u, a * cb + kb)),
            pl.BlockSpec((1, W, gw), lambda b, a, u: (b, prev(u), a * cb + vb)),
            pl.BlockSpec((1, M * W, gw), lambda b, a, u: (b, u, a * cb + vb)),
            pl.BlockSpec((2, 2, W, 2 * W), lambda b, a, u: (0, 0, 0, 0)),
        ],
        out_specs=[pl.BlockSpec((1, M * W, gw), lambda b, a, u: (b, u, a))] * 2,
        compiler_params=_cparams(("arbitrary",) * 3),
        name=f"dilated_attention_g{g}",
    )(pv, pv, pv, pv, pv, bias)
    return [o.reshape(bsz * seq, 2 * LANES) for o in out]


def _t5_bucket(dist):
    max_exact = N_BUCKETS // 2
    d = jnp.maximum(dist, 0)
    df = jnp.maximum(d, 1).astype(F32)
    large = max_exact + (jnp.log(df / max_exact) / math.log(MAX_DISTANCE / max_exact)
                         * (N_BUCKETS - max_exact)).astype(I32)
    large = jnp.minimum(large, N_BUCKETS - 1)
    return jnp.where(d < max_exact, d, large)


def _bucket_bias(dist, table):
    onehot = (_t5_bucket(dist)[..., None] == jnp.arange(N_BUCKETS, dtype=I32)).astype(F32)
    out = jnp.einsum('...b,bh->h...', onehot, table.astype(F32), precision=lax.Precision.HIGHEST)
    return out


def _dil_bias_tiles(rel_bias, g, window, dil, W=128):
    assert window // dil == W
    i = np.arange(W)[:, None]
    j = np.arange(2 * W)[None, :]
    n = i + W - j
    valid = (n >= 0) & (n <= W)
    heads = rel_bias[:, g * N_DIL_SLOTS:(g + 1) * N_DIL_SLOTS]
    tiles = jnp.where(valid[None], _bucket_bias(jnp.asarray(np.clip(n, 0, W) * dil, I32), heads), NEG)
    return tiles.reshape(2, 2, W, 2 * W)


def _oproj0_kernel(fox_ref, o0_ref, o1_ref, o2_ref, l0_ref, l1_ref, l2_ref, w_ref, x_ref, gate_ref, out_ref):
    ls = [l0_ref[...], l1_ref[...], l2_ref[...]]
    os_ = [o0_ref[...], o1_ref[...], o2_ref[...]]
    mx = jnp.maximum(jnp.maximum(ls[0], ls[1]), ls[2])
    ws = [jnp.exp(l - mx) for l in ls]
    den = ws[0] + ws[1] + ws[2]
    dil = (ws[0] * os_[0] + ws[1] * os_[1] + ws[2] * os_[2]) / den
    nf = fox_ref.shape[1]
    mix = _dot(fox_ref[...], w_ref[:nf, :]) + _dot(dil.astype(BF16), w_ref[nf:, :])
    out_ref[...] = x_ref[...] + gate_ref[0] * mix


def out_proj_even(fox, dil_o, dil_l, w, x2, gate, seq, tm=512):
    n_tok, d = x2.shape
    tm = min(tm, seq)
    nf = fox.shape[1]
    nd = N_DIL_SLOTS * HEAD_DIM
    row = lambda i: (i, 0)
    return pl.pallas_call(
        _oproj0_kernel,
        out_shape=jax.ShapeDtypeStruct((n_tok, d), F32),
        grid=(n_tok // tm,),
        in_specs=[pl.BlockSpec((tm, nf), row)]
                 + [pl.BlockSpec((tm, nd), row)] * 6
                 + [pl.BlockSpec((nf + nd, d), lambda i: (0, 0)),
                    pl.BlockSpec((tm, d), row),
                    pl.BlockSpec((1, 1, d), lambda i: (i * tm // seq, 0, 0))],
        out_specs=pl.BlockSpec((tm, d), row),
        compiler_params=_cparams(("arbitrary",), VMEM_LIMIT_BYTES),
        name="out_proj_even",
    )(fox, *dil_o, *dil_l, w, x2, gate.reshape(-1, 1, d))


def _oproj_kernel(a_ref, w_ref, x_ref, gate_ref, out_ref):
    out_ref[...] = x_ref[...] + gate_ref[0] * _dot(a_ref[...], w_ref[...])


def out_proj_odd(a, w, x2, gate, seq, tm=512):
    n_tok, d = x2.shape
    tm = min(tm, seq)
    ka = a.shape[1]
    row = lambda i: (i, 0)
    return pl.pallas_call(
        _oproj_kernel,
        out_shape=jax.ShapeDtypeStruct((n_tok, d), F32),
        grid=(n_tok // tm,),
        in_specs=[pl.BlockSpec((tm, ka), row),
                  pl.BlockSpec((ka, d), lambda i: (0, 0)),
                  pl.BlockSpec((tm, d), row),
                  pl.BlockSpec((1, 1, d), lambda i: (i * tm // seq, 0, 0))],
        out_specs=pl.BlockSpec((tm, d), row),
        compiler_params=_cparams(("arbitrary",), VMEM_LIMIT_BYTES),
        name="out_proj_odd",
    )(a, w, x2, gate.reshape(-1, 1, d))


def _router_kernel(x_ref, g_ref, sc_ref, sh_ref, rw_ref, rb_ref,
                   h_ref, ti_ref, tg_ref, rk_ref, cnt_ref, carry):
    i = pl.program_id(0)

    @pl.when(i == 0)
    def _():
        carry[...] = jnp.zeros_like(carry)

    tm = x_ref.shape[0]
    x = x_ref[...]
    ms = jnp.mean(x * x, axis=-1, keepdims=True)
    y = x * lax.rsqrt(ms + EPS) * g_ref[...]
    hr = (y * (1.0 + sc_ref[0]) + sh_ref[0]).astype(BF16).astype(F32)
    bits = pltpu.bitcast(hr, I32)
    half = hr.shape[1] // 2
    h_ref[...] = lax.shift_right_logical(bits[:, :half], 16) | (bits[:, half:] & jnp.int32(-65536))
    logits = _dot_f32(hr, rw_ref[...]) + rb_ref[...]
    lane = lax.broadcasted_iota(I32, (tm, LANES), 1)
    l = logits
    vals, idxs, hots = [], [], []
    for _ in range(TOP_K):
        mx = jnp.max(l, axis=1, keepdims=True)
        idx = jnp.min(jnp.where(l == mx, lane, LANES), axis=1, keepdims=True)
        hot = lane == idx
        l = jnp.where(hot, -jnp.inf, l)
        vals.append(mx)
        idxs.append(idx)
        hots.append(hot)
    es = [jnp.exp(v - vals[0]) for v in vals]
    den = es[0] + es[1] + es[2] + es[3]
    member = jnp.zeros((tm, LANES), F32)
    for hot in hots:
        member = member + jnp.where(hot, 1.0, 0.0)
    row = lax.broadcasted_iota(I32, (tm, tm), 0)
    col = lax.broadcasted_iota(I32, (tm, tm), 1)
    tri = jnp.where(row > col, 1.0, 0.0).astype(BF16)
    before = _dot(tri, member.astype(BF16)) + carry[...]
    carry[...] = carry[...] + jnp.sum(member, axis=0, keepdims=True)
    ti = jnp.zeros((tm, LANES), I32)
    tg = jnp.zeros((tm, LANES), F32)
    rk = jnp.zeros((tm, LANES), I32)
    for k in range(TOP_K):
        rank = jnp.sum(jnp.where(hots[k], before, 0.0), axis=1, keepdims=True).astype(I32)
        ti = jnp.where(lane == k, idxs[k], ti)
        tg = jnp.where(lane == k, es[k] / den, tg)
        rk = jnp.where(lane == k, rank, rk)
    ti_ref[...] = ti
    tg_ref[...] = tg
    rk_ref[...] = rk
    cnt_ref[...] = jnp.broadcast_to(carry[...], cnt_ref.shape)


def moe_router(x2, g, scale, shift, rw, rb, seq, tm=256):
    n_tok, d = x2.shape
    tm = min(tm, seq)
    rwp = jnp.zeros((d, LANES), F32).at[:, :N_EXPERTS].set(rw)
    rbp = jnp.full((1, LANES), NEG, F32).at[0, :N_EXPERTS].set(rb)
    row = lambda i: (i, 0)
    bsel = lambda i: (i * tm // seq, 0, 0)
    return pl.pallas_call(
        _router_kernel,
        out_shape=[jax.ShapeDtypeStruct((n_tok, d // 2), I32),
                   jax.ShapeDtypeStruct((n_tok, LANES), I32),
                   jax.ShapeDtypeStruct((n_tok, LANES), F32),
                   jax.ShapeDtypeStruct((n_tok, LANES), I32),
                   jax.ShapeDtypeStruct((8, LANES), F32)],
        grid=(n_tok // tm,),
        in_specs=[pl.BlockSpec((tm, d), row),
                  pl.BlockSpec((1, d), lambda i: (0, 0)),
                  pl.BlockSpec((1, 1, d), bsel),
                  pl.BlockSpec((1, 1, d), bsel),
                  pl.BlockSpec((d, LANES), lambda i: (0, 0)),
                  pl.BlockSpec((1, LANES), lambda i: (0, 0))],
        out_specs=[pl.BlockSpec((tm, d // 2), row),
                   pl.BlockSpec((tm, LANES), row),
                   pl.BlockSpec((tm, LANES), row),
                   pl.BlockSpec((tm, LANES), row),
                   pl.BlockSpec((8, LANES), lambda i: (0, 0))],
        scratch_shapes=[pltpu.VMEM((1, LANES), F32)],
        compiler_params=_cparams(("arbitrary",)),
        name="moe_router",
    )(x2, g.reshape(1, d), scale.reshape(-1, 1, d), shift.reshape(-1, 1, d), rwp, rbp)


def _dispatch_kernel(dest_ref, h_ref, xs_in_ref, xs_ref, sem):
    del xs_in_ref
    tm = h_ref.shape[0]

    def row_copy(i, d):
        return pltpu.make_async_copy(h_ref.at[pl.ds(i, 1)], xs_ref.at[pl.ds(d, 1)], sem)

    def issue(i, c):
        for k in range(TOP_K):
            row_copy(i, dest_ref[i * TOP_K + k]).start()
        return c

    lax.fori_loop(0, tm, issue, 0)

    def drain(i, c):
        for k in range(TOP_K):
            row_copy(0, 0).wait()
        return c

    lax.fori_loop(0, tm, drain, 0)


def moe_dispatch(h32, dest_flat, rows, tm=256):
    n_tok, dw = h32.shape
    tm = min(tm, n_tok)
    xs0 = jnp.zeros((rows, dw), I32)
    return pl.pallas_call(
        _dispatch_kernel,
        out_shape=jax.ShapeDtypeStruct((rows, dw), I32),
        grid=(n_tok // tm,),
        in_specs=[pl.BlockSpec((tm * TOP_K,), lambda i: (i,), memory_space=pltpu.SMEM),
                  pl.BlockSpec((tm, dw), lambda i: (i, 0)),
                  pl.BlockSpec(memory_space=pl.ANY)],
        out_specs=pl.BlockSpec(memory_space=pl.ANY),
        scratch_shapes=[pltpu.SemaphoreType.DMA(())],
        input_output_aliases={2: 0},
        compiler_params=_cparams(("arbitrary",)),
        name="moe_dispatch",
    )(dest_flat, h32, xs0)


def _experts_kernel(ce_ref, nu_ref, x_ref, w1_ref, b1_ref, w2_ref, b2_ref, y_ref, act_scr, *, tn):
    c = pl.program_id(0)
    de = act_scr.shape[1]

    @pl.when(c < nu_ref[0])
    def _():
        words = x_ref[...]
        x = jnp.concatenate([pltpu.bitcast(words << 16, F32),
                             pltpu.bitcast(words & jnp.int32(-65536), F32)], axis=1).astype(BF16)
        for j in range(de // tn):
            sl = slice(j * tn, (j + 1) * tn)
            sl2 = slice(de + j * tn, de + (j + 1) * tn)
            glu = _dot(x, w1_ref[0, :, sl]) + b1_ref[0, :, sl]
            lin = _dot(x, w1_ref[0, :, sl2]) + b1_ref[0, :, sl2]
            glu = jnp.minimum(glu, SWIGLU_LIMIT)
            lin = jnp.clip(lin, -SWIGLU_LIMIT, SWIGLU_LIMIT)
            act_scr[:, sl] = (glu * jax.nn.sigmoid(SWIGLU_ALPHA * glu) * (lin + 1.0)).astype(BF16)
        a = act_scr[...]
        d = y_ref.shape[1]
        for j in range(d // tn):
            sl = slice(j * tn, (j + 1) * tn)
            y_ref[:, sl] = _dot(a, w2_ref[0, :, sl]) + b2_ref[0, :, sl]

    @pl.when(c >= nu_ref[0])
    def _():
        y_ref[...] = jnp.zeros_like(y_ref)


def moe_experts(xs, chunk_exp, n_used, w1, b1, w2, b2, tn=256):
    rows = xs.shape[0]
    ne, d, de2 = w1.shape
    de = de2 // 2
    n_chunks = rows // MOE_CHUNK
    grid_spec = pltpu.PrefetchScalarGridSpec(
        num_scalar_prefetch=2,
        grid=(n_chunks,),
        in_specs=[pl.BlockSpec((MOE_CHUNK, d // 2), lambda c, ce, nu: (c, 0)),
                  pl.BlockSpec((1, d, de2), lambda c, ce, nu: (ce[c], 0, 0)),
                  pl.BlockSpec((1, 1, de2), lambda c, ce, nu: (ce[c], 0, 0)),
                  pl.BlockSpec((1, de, d), lambda c, ce, nu: (ce[c], 0, 0)),
                  pl.BlockSpec((1, 1, d), lambda c, ce, nu: (ce[c], 0, 0))],
        out_specs=pl.BlockSpec((MOE_CHUNK, d), lambda c, ce, nu: (c, 0)),
        scratch_shapes=[pltpu.VMEM((MOE_CHUNK, de), BF16)],
    )
    return pl.pallas_call(
        functools.partial(_experts_kernel, tn=min(tn, de)),
        out_shape=jax.ShapeDtypeStruct((rows, d), F32),
        grid_spec=grid_spec,
        compiler_params=_cparams(("arbitrary",), VMEM_LIMIT_BYTES),
        name="moe_experts",
    )(chunk_exp, n_used, xs, w1, b1.reshape(ne, 1, de2), w2, b2.reshape(ne, 1, d))


def _combine_kernel(dest_ref, y_ref, tg_ref, x_ref, gate_ref, o_ref, buf, sem):
    tm = x_ref.shape[0]

    def row_copy(i, k, d):
        return pltpu.make_async_copy(y_ref.at[pl.ds(d, 1)], buf.at[k, pl.ds(i, 1)], sem)

    def issue(i, c):
        for k in range(TOP_K):
            row_copy(i, k, dest_ref[i * TOP_K + k]).start()
        return c

    lax.fori_loop(0, tm, issue, 0)

    def drain(i, c):
        for k in range(TOP_K):
            row_copy(0, 0, 0).wait()
        return c

    lax.fori_loop(0, tm, drain, 0)
    tg = tg_ref[...]
    moe = tg[:, 0:1] * buf[0]
    for k in range(1, TOP_K):
        moe = moe + tg[:, k:k + 1] * buf[k]
    o_ref[...] = x_ref[...] + gate_ref[0] * moe


def moe_combine(y, dest_flat, tg, x2, gate, seq, tm=256):
    n_tok, d = x2.shape
    tm = min(tm, seq)
    row = lambda i: (i, 0)
    return pl.pallas_call(
        _combine_kernel,
        out_shape=jax.ShapeDtypeStruct((n_tok, d), F32),
        grid=(n_tok // tm,),
        in_specs=[pl.BlockSpec((tm * TOP_K,), lambda i: (i,), memory_space=pltpu.SMEM),
                  pl.BlockSpec(memory_space=pl.ANY),
                  pl.BlockSpec((tm, LANES), row),
                  pl.BlockSpec((tm, d), row),
                  pl.BlockSpec((1, 1, d), lambda i: (i * tm // seq, 0, 0))],
        out_specs=pl.BlockSpec((tm, d), row),
        scratch_shapes=[pltpu.VMEM((TOP_K, tm, d), F32), pltpu.SemaphoreType.DMA(())],
        compiler_params=_cparams(("arbitrary",), VMEM_LIMIT_BYTES),
        name="moe_combine",
    )(dest_flat, y, tg, x2, gate.reshape(-1, 1, d))


def moe_block(x2, g, scale, shift, gate, rw, rb, w1, b1, w2, b2, seq):
    n_tok, d = x2.shape
    h, ti, tg, rk, cnt = moe_router(x2, g, scale, shift, rw, rb, seq)
    counts = cnt[0, :N_EXPERTS].astype(I32)
    padded = ((counts + MOE_CHUNK - 1) // MOE_CHUNK) * MOE_CHUNK
    ends_pad = jnp.cumsum(padded)
    start_pad = ends_pad - padded
    n_chunks = -(-(n_tok * TOP_K) // MOE_CHUNK) + N_EXPERTS
    rows = n_chunks * MOE_CHUNK
    experts = jnp.arange(N_EXPERTS, dtype=I32)
    slot_start = jnp.sum(jnp.where(ti[:, :TOP_K, None] == experts, start_pad, 0), axis=-1)
    dest = (slot_start + rk[:, :TOP_K]).reshape(-1)
    chunk_pos = jnp.arange(n_chunks, dtype=I32)[:, None] * MOE_CHUNK
    chunk_exp = jnp.minimum(jnp.sum((ends_pad[None, :] <= chunk_pos).astype(I32), axis=1), N_EXPERTS - 1)
    n_used = (ends_pad[-1:] // MOE_CHUNK).astype(I32)
    xs = moe_dispatch(h, dest, rows)
    y = moe_experts(xs, chunk_exp, n_used, w1.astype(BF16), b1, w2.astype(BF16), b2)
    return moe_combine(y, dest, tg, x2, gate, seq)


def _dsa_index_kernel(qi_ref, k2_ref, wi_ref, mask_ref, ka_scr, kb_scr, key_scr,
                      *, S, KC, n_sel, idx_scale):
    qb = pl.program_id(1)
    R = qi_ref.shape[1]
    nchunks = S // KC
    lane = lax.broadcasted_iota(I32, (1, LANES), 1)

    @pl.when(qb == 0)
    def _():
        kk = k2_ref[0]
        ka_scr[...] = jnp.where(lane < HEAD_DIM, kk, 0.0).astype(BF16)
        kb_scr[...] = jnp.where(lane >= HEAD_DIM, kk, 0.0).astype(BF16)

    nck = (qb * R + R + KC - 1) // KC
    w = wi_ref[0].astype(BF16).astype(F32)
    qv = qi_ref[0].astype(BF16)

    def relu_bf16(d):
        return jnp.maximum(d, 0.0).astype(BF16).astype(F32)

    row = qb * R + lax.broadcasted_iota(I32, (R, KC), 0)
    col0 = lax.broadcasted_iota(I32, (R, KC), 1)

    def sort_key(x):
        bits = pltpu.bitcast(x, I32)
        return jnp.where(bits < 0, bits ^ jnp.int32(0x7FFFFFFF), bits)

    def score_chunk(c, gmax):
        off = pl.multiple_of(c * KC, KC)
        ka = ka_scr[pl.ds(off, KC), :]
        kb = kb_scr[pl.ds(off, KC), :]
        sc = jnp.zeros((R, KC), F32)
        for pp in range(N_IDX_HEADS // 2):
            qp = qv[:, pp * LANES:(pp + 1) * LANES]
            sc = sc + w[:, 2 * pp:2 * pp + 1] * relu_bf16(_dot_nt(qp, ka))
            sc = sc + w[:, 2 * pp + 1:2 * pp + 2] * relu_bf16(_dot_nt(qp, kb))
        sc = sc * idx_scale
        sc = jnp.where(sc == 0.0, 0.0, sc)
        sc = jnp.where(off + col0 <= row, sc, -jnp.inf)
        key_scr[c] = jnp.where(sc == -jnp.inf, INT_MIN, sort_key(sc))
        g0, g1 = gmax
        for j in range(0, KC // LANES, 2):
            g0 = jnp.maximum(g0, sc[:, j * LANES:(j + 1) * LANES])
            g1 = jnp.maximum(g1, sc[:, (j + 1) * LANES:(j + 2) * LANES])
        return g0, g1

    neg_inf = jnp.full((R, LANES), -jnp.inf, F32)
    g0, g1 = lax.fori_loop(0, nck, score_chunk, (neg_inf, neg_inf))
    assert n_sel <= 2 * LANES
    lo_f = jnp.min(jnp.minimum(g0, g1), axis=1, keepdims=True)
    hi_f = jnp.max(jnp.maximum(g0, g1), axis=1, keepdims=True)
    lo_key = jnp.broadcast_to(jnp.where(lo_f == -jnp.inf, INT_MIN, sort_key(lo_f)), (R, LANES))
    hi_key = jnp.broadcast_to(jnp.where(hi_f == -jnp.inf, INT_MIN, sort_key(hi_f)), (R, LANES))
    shared = jnp.min(lax.clz(lo_key ^ hi_key).astype(F32)).astype(I32)
    prefix_mask = jnp.where(shared == 0, 0, jnp.int32(-1) << (32 - jnp.maximum(shared, 1)))
    start = jnp.where(shared == 0, jnp.int32(INT_MIN), lo_key & prefix_mask)

    def count(pred):
        def body(c, acc):
            kc = key_scr[c]
            for j in range(KC // LANES):
                acc = acc + jnp.where(pred(kc[:, j * LANES:(j + 1) * LANES], c * KC + j * LANES), 1.0, 0.0)
            return acc
        acc = lax.fori_loop(0, nck, body, jnp.zeros((R, LANES), F32))
        return jnp.broadcast_to(jnp.sum(acc, axis=1, keepdims=True), (R, LANES))

    nsel_f = float(n_sel)

    def bit_step(t, carry):
        cur, cnt_cur = carry
        cand = cur + (jnp.int32(1) << (31 - t))
        cnt = count(lambda kc, base: kc >= cand)
        ok = cnt >= nsel_f
        return jnp.where(ok, cand, cur), jnp.where(ok, cnt, cnt_cur)

    tau, cnt_tau = lax.fori_loop(shared, 32, bit_step, (start, jnp.full((R, LANES), float(S), F32)))

    overflow = jnp.logical_and(tau > INT_MIN, cnt_tau > nsel_f)
    any_over = jnp.max(jnp.where(overflow, 1.0, 0.0)) > 0.0
    lane_r = lax.broadcasted_iota(I32, (R, LANES), 1)

    def write_mask(keep_fn):
        for c in range(nchunks):
            @pl.when(c < nck)
            def _():
                kc = key_scr[c]
                for j in range(KC // LANES):
                    keep = keep_fn(kc[:, j * LANES:(j + 1) * LANES], c * KC + j * LANES + lane_r)
                    mask_ref[0, :, c * KC + j * LANES:c * KC + (j + 1) * LANES] = (
                        jnp.where(keep, 1, 0).astype(jnp.int8))

            @pl.when(c >= nck)
            def _():
                mask_ref[0, :, c * KC:(c + 1) * KC] = jnp.zeros((R, KC), jnp.int8)

    @pl.when(jnp.logical_not(any_over))
    def _():
        floor = jnp.maximum(tau, INT_MIN + 1)
        write_mask(lambda kj, pos: kj >= floor)

    @pl.when(any_over)
    def _():
        need = nsel_f - count(lambda kc, base: kc > tau)

        def jstep(t, cur):
            cand = cur + (jnp.int32(1) << (S.bit_length() - 2 - t))
            cnt = count(lambda kc, base: jnp.logical_and(kc == tau, base + lane_r < cand))
            return jnp.where(cnt < need, cand, cur)

        jlim = lax.fori_loop(0, S.bit_length() - 1, jstep, jnp.zeros((R, LANES), I32))
        write_mask(lambda kj, pos: jnp.logical_and(
            jnp.logical_or(kj > tau, jnp.logical_and(kj == tau, pos <= jlim)), kj > INT_MIN))


def dsa_index_mask(aux, qcol, kcol, wcol, n_sel, R=128, KC=512):
    bsz, seq, _ = aux.shape
    R = min(R, seq)
    KC = min(KC, seq)
    nq = N_IDX_HEADS * HEAD_DIM
    return pl.pallas_call(
        functools.partial(_dsa_index_kernel, S=seq, KC=KC, n_sel=n_sel,
                          idx_scale=float(nq ** -0.5)),
        out_shape=jax.ShapeDtypeStruct((bsz, seq, seq), jnp.int8),
        grid=(bsz, seq // R),
        in_specs=[pl.BlockSpec((1, R, nq), lambda b, q: (b, q, qcol // nq)),
                  pl.BlockSpec((1, seq, LANES), lambda b, q: (b, 0, kcol // LANES)),
                  pl.BlockSpec((1, R, LANES), lambda b, q: (b, q, wcol // LANES))],
        out_specs=pl.BlockSpec((1, R, seq), lambda b, q: (b, q, 0)),
        scratch_shapes=[pltpu.VMEM((seq, LANES), BF16), pltpu.VMEM((seq, LANES), BF16),
                        pltpu.VMEM((seq // KC, R, KC), I32)],
        compiler_params=_cparams(("arbitrary", "arbitrary"), VMEM_LIMIT_BYTES),
        name="dsa_index_mask",
    )(aux, aux, aux)


def _dsa_attn_kernel(qi_ref, ki_ref, q_ref, k_ref, v_ref, mask_ref, tz_ref, o_ref,
                     m_scr, l_scr, acc_scr, *, TQ, TK, DCL):
    s = pl.program_id(1)
    qi = qi_ref[s]
    ki = ki_ref[s]
    k_last = ((qi + 1) * TQ - 1) // TK

    @pl.when(ki == 0)
    def _():
        m_scr[...] = jnp.full_like(m_scr, NEG)
        l_scr[...] = jnp.zeros_like(l_scr)
        acc_scr[...] = jnp.zeros_like(acc_scr)

    lane = lax.broadcasted_iota(I32, (1, LANES), 1)
    lo_half = lane < HEAD_DIM
    d0 = qi * TQ - ki * TK
    cb0 = (DCL - jnp.minimum(d0, DCL)) // LANES
    madd = jnp.where(mask_ref[0].astype(I32) != 0, 0.0, NEG).astype(BF16)
    npair = q_ref.shape[2] // LANES

    def logits(h):
        cs = slice((h // 2) * LANES, (h // 2 + 1) * LANES)
        qsel = lo_half if h % 2 == 0 else jnp.logical_not(lo_half)
        q = _keep_lanes(q_ref[0, :, cs] * QK_SCALE, qsel)
        bias = jnp.concatenate([tz_ref[h, cb0 + jj] for jj in range(TK // LANES)], axis=1)
        return _dot_nt(q, k_ref[0, :, cs]).astype(BF16) + bias + madd

    def weighted_values(h, pm):
        return _dot(pm, v_ref[0, :, h * LANES:(h + 1) * LANES])

    def accumulate(pp, alphas, pvs):
        cs = slice(pp * LANES, (pp + 1) * LANES)
        acc_scr[:, cs] = (acc_scr[:, cs] * jnp.where(lo_half, alphas[0], alphas[1])
                          + jnp.where(lo_half, pvs[0], pvs[1]))
        l_scr[pp] = (l_scr[pp] * jnp.where(lo_half, alphas[1], alphas[0])
                     + jnp.where(lo_half, pvs[1], pvs[0]))

    ahead = 2
    nh = 2 * npair
    queue = [logits(h) for h in range(ahead)]
    alphas, pvs, probs = {}, {}, {}
    for h in range(nh + 1):
        if h < nh:
            sc = queue.pop(0)
            if h + ahead < nh:
                queue.append(logits(h + ahead))
            m_scr[h], alphas[h], probs[h] = _online_softmax(sc, m_scr[h])
        if h >= 1:
            pvs[h - 1] = weighted_values(h - 1, probs.pop(h - 1))
            if (h - 1) % 2 == 1:
                accumulate((h - 1) // 2, (alphas[h - 2], alphas[h - 1]), (pvs[h - 2], pvs[h - 1]))

    @pl.when(ki == k_last)
    def _():
        for pp in range(npair):
            cs = slice(pp * LANES, (pp + 1) * LANES)
            l = pltpu.roll(l_scr[pp], HEAD_DIM, axis=1)
            o_ref[0, :, cs] = (acc_scr[:, cs] / l).astype(o_ref.dtype)


def _dsa_bias_table(rel_bias, seq, TQ, TK, DCL):
    i = np.arange(TQ)[:, None]
    c = np.arange(DCL + TK)[None, :]
    d = np.clip(i - c + DCL, 0, seq - 1)
    tz = _bucket_bias(jnp.asarray(d, I32), rel_bias)
    tz = tz.reshape(rel_bias.shape[1], TQ, (DCL + TK) // LANES, LANES)
    return tz.transpose(0, 2, 1, 3).astype(BF16)


def _np_t5_bucket(d):
    max_exact = N_BUCKETS // 2
    df = np.maximum(d, 1).astype(np.float32)
    large = max_exact + (np.log(df / max_exact) / math.log(MAX_DISTANCE / max_exact)
                         * (N_BUCKETS - max_exact)).astype(np.int32)
    return np.where(d < max_exact, d, np.minimum(large, N_BUCKETS - 1))


def dsa_attention(q_src, qcol, kv, mask, rel_bias, TQ=256, TK=1024):
    bsz, seq, _ = q_src.shape
    TQ = min(TQ, seq)
    TK = min(TK, seq)
    hw = N_DSA_HEADS * HEAD_DIM
    buckets = _np_t5_bucket(np.arange(seq))
    first_const = int(np.max(np.nonzero(buckets != buckets[-1])[0])) + 1
    DCL = min(seq, -(-(first_const + TK - 1) // LANES) * LANES)
    tz = _dsa_bias_table(rel_bias, seq, TQ, TK, DCL)
    qs, ks = _tri_schedule(seq // TQ, TK // TQ)
    grid_spec = pltpu.PrefetchScalarGridSpec(
        num_scalar_prefetch=2,
        grid=(bsz, qs.shape[0]),
        in_specs=[
            pl.BlockSpec((1, TQ, hw), lambda b, s, qi, ki: (b, qi[s], qcol // hw)),
            pl.BlockSpec((1, TK, hw), lambda b, s, qi, ki: (b, ki[s], 0)),
            pl.BlockSpec((1, TK, 2 * hw), lambda b, s, qi, ki: (b, ki[s], 0)),
            pl.BlockSpec((1, TQ, TK), lambda b, s, qi, ki: (b, qi[s], ki[s])),
            pl.BlockSpec(tz.shape, lambda b, s, qi, ki: (0, 0, 0, 0), pipeline_mode=pl.Buffered(1)),
        ],
        out_specs=pl.BlockSpec((1, TQ, hw), lambda b, s, qi, ki: (b, qi[s], 0)),
        scratch_shapes=[pltpu.VMEM((N_DSA_HEADS, TQ, LANES), F32),
                        pltpu.VMEM((N_DSA_HEADS // 2, TQ, LANES), F32),
                        pltpu.VMEM((TQ, hw), F32)],
    )
    return pl.pallas_call(
        functools.partial(_dsa_attn_kernel, TQ=TQ, TK=TK, DCL=DCL),
        out_shape=jax.ShapeDtypeStruct((bsz, seq, hw), BF16),
        grid_spec=grid_spec,
        compiler_params=_cparams(("arbitrary", "arbitrary"), VMEM_LIMIT_BYTES),
        name="dsa_attention",
    )(qs, ks, q_src, kv, _values_with_ones(kv[:, :, hw:], N_DSA_HEADS), mask, tz)


def _rms_kernel(x_ref, g_ref, o_ref):
    x = x_ref[...]
    ms = jnp.mean(x * x, axis=-1, keepdims=True)
    o_ref[...] = x * lax.rsqrt(ms + EPS) * g_ref[...]


def final_rmsnorm(x2, g, tm=512):
    n_tok, d = x2.shape
    tm = min(tm, n_tok)
    return pl.pallas_call(
        _rms_kernel,
        out_shape=jax.ShapeDtypeStruct((n_tok, d), F32),
        grid=(n_tok // tm,),
        in_specs=[pl.BlockSpec((tm, d), lambda i: (i, 0)), pl.BlockSpec((1, d), lambda i: (0, 0))],
        out_specs=pl.BlockSpec((tm, d), lambda i: (i, 0)),
        compiler_params=_cparams(("arbitrary",)),
        name="final_rmsnorm",
    )(x2, g.reshape(1, d))


def even_mixer_residual(x2, shift1, scale1, gate1, norm1, w_in, fox_fb, w_out, rel_bias, bsz, seq):
    d = x2.shape[1]
    nfq = 3 * N_FOX_HEADS * HEAD_DIM
    w_main = jnp.concatenate([w_in[:, :nfq], w_in[:, nfq + N_FOX_HEADS:]], axis=1).astype(BF16)
    rep = np.repeat(np.arange(N_FOX_HEADS), FOX_PARTS)
    gcols = np.concatenate([rep, rep])
    w_gate = jnp.zeros((d, LANES), F32).at[:, :2 * FOX_XW].set(w_in[:, nfq + gcols])
    fb = jnp.zeros((1, LANES), F32).at[0, :2 * FOX_XW].set(fox_fb[gcols])
    proj, gate_z = norm_mod_matmul(x2, norm1, scale1, shift1, w_main, w_gate, seq)
    c = proj.shape[1]
    proj3 = proj.reshape(bsz, seq, c)
    qx, kx = fox_gate_columns(gate_z.reshape(bsz, seq, LANES), fb)
    fox = fox_attention(proj3, qx, kx, N_FOX_HEADS)
    dil_o, dil_l = [], []
    for g, (window, dil) in enumerate(DIL_PAIRS):
        bias = _dil_bias_tiles(rel_bias, g, window, dil)
        o, l = dilated_group_attention(proj3, bias, g, dil, nfq + g * N_DIL_SLOTS * HEAD_DIM)
        dil_o.append(o)
        dil_l.append(l)
    return out_proj_even(fox.reshape(bsz * seq, -1), dil_o, dil_l, w_out.astype(BF16), x2, gate1, seq)


def odd_mixer_residual(x2, shift1, scale1, gate1, norm1, w_in, kv_norm, w_ukv, w_out, rel_bias, bsz, seq):
    d = x2.shape[1]
    nq = N_DSA_HEADS * HEAD_DIM
    ni = N_IDX_HEADS * HEAD_DIM
    c_ckv, c_qi = nq, nq + KV_RANK
    c_ki = c_qi + ni
    c_wi = c_ki + HEAD_DIM
    w_main = w_in[:, :nq].astype(BF16)
    c_w, c_k, c_q = KV_RANK, KV_RANK + LANES, KV_RANK + 2 * LANES
    w_aux = jnp.zeros((d, c_q + ni), F32)
    w_aux = w_aux.at[:, :KV_RANK].set(w_in[:, c_ckv:c_qi])
    w_aux = w_aux.at[:, c_w:c_w + N_IDX_HEADS].set(w_in[:, c_wi:])
    w_aux = w_aux.at[:, c_k:c_k + HEAD_DIM].set(w_in[:, c_ki:c_wi])
    w_aux = w_aux.at[:, c_k + HEAD_DIM:c_q].set(w_in[:, c_ki:c_wi])
    w_aux = w_aux.at[:, c_q:].set(w_in[:, c_qi:c_ki])
    proj, aux = norm_mod_matmul(x2, norm1, scale1, shift1, w_main, w_aux, seq)
    zeros = jnp.zeros((bsz, KV_RANK), F32)
    kv = norm_mod_matmul(aux[:, :KV_RANK], kv_norm, zeros, zeros, w_ukv.astype(BF16), None, seq)
    proj3 = proj.reshape(bsz, seq, -1)
    n_sel = min(DSA_TOPK, seq // 4)
    mask = dsa_index_mask(aux.reshape(bsz, seq, -1), c_q, c_k, c_w, n_sel)
    att = dsa_attention(proj3, 0, kv.reshape(bsz, seq, -1), mask, rel_bias)
    return out_proj_odd(att.reshape(bsz * seq, -1), w_out.astype(BF16), x2, gate1, seq)


def kernel(x, c, rel_bias, l0_norm1, l0_ada_w, l0_ada_b, l0_w_in, l0_fox_fb, l0_w_out, l0_norm2, l0_router_w, l0_router_b, l0_w1, l0_b1, l0_w2, l0_b2, l1_norm1, l1_ada_w, l1_ada_b, l1_w_in, l1_kv_norm, l1_w_ukv, l1_w_out, l1_norm2, l1_router_w, l1_router_b, l1_w1, l1_b1, l1_w2, l1_b2, final_norm):
    bsz, seq, d = x.shape
    x2 = x.reshape(bsz * seq, d)
    layers = (
        (l0_norm1, l0_ada_w, l0_ada_b, l0_norm2, l0_router_w, l0_router_b, l0_w1, l0_b1, l0_w2, l0_b2),
        (l1_norm1, l1_ada_w, l1_ada_b, l1_norm2, l1_router_w, l1_router_b, l1_w1, l1_b1, l1_w2, l1_b2),
    )
    for i, (norm1, ada_w, ada_b, norm2, rw, rb, w1, b1, w2, b2) in enumerate(layers):
        mods = ada_mods(c, ada_w, ada_b)
        shift1, scale1, gate1, shift2, scale2, gate2 = (mods[:, j * d:(j + 1) * d] for j in range(6))
        if i % 2 == 0:
            x2 = even_mixer_residual(x2, shift1, scale1, gate1, norm1, l0_w_in, l0_fox_fb, l0_w_out,
                                     rel_bias, bsz, seq)
        else:
            x2 = odd_mixer_residual(x2, shift1, scale1, gate1, norm1, l1_w_in, l1_kv_norm, l1_w_ukv,
                                    l1_w_out, rel_bias, bsz, seq)
        x2 = moe_block(x2, norm2, scale2, shift2, gate2, rw, rb, w1, b1, w2, b2, seq)
    return final_rmsnorm(x2, final_norm).reshape(bsz, seq, d)
```

```python
import functools
import math

import numpy as np
import jax
import jax.numpy as jnp
from jax import lax
from jax.experimental import pallas as pl
from jax.experimental.pallas import tpu as pltpu

F32, BF16, I32 = jnp.float32, jnp.bfloat16, jnp.int32

LANES = 128
VMEM_LIMIT_BYTES = 56 * 1024 * 1024

HEAD_DIM = 64
N_FOX_HEADS = 8
DIL_PAIRS = ((128, 1), (512, 4), (2048, 16))
N_DIL_SLOTS = 4
N_DSA_HEADS = 16
KV_RANK = 256
N_IDX_HEADS = 8
DSA_TOPK = 256
N_BUCKETS = 32
MAX_DISTANCE = 2048
N_EXPERTS = 32
TOP_K = 4
D_EXPERT = 1024
SWIGLU_ALPHA = 1.702
SWIGLU_LIMIT = 7.0
MOE_CHUNK = 256
EPS = 1e-6

NEG = -1e30
INT_MIN = -(2 ** 31)
QK_SCALE = HEAD_DIM ** -0.5
FOX_PARTS = 3
FOX_XW = FOX_PARTS * N_FOX_HEADS


def _cparams(sem, vmem=None):
    return pltpu.CompilerParams(dimension_semantics=sem, vmem_limit_bytes=vmem)


def _dot_nt(a, b):
    return lax.dot_general(a, b, (((1,), (1,)), ((), ())), preferred_element_type=F32)


def _dot(a, b):
    return jnp.dot(a, b, preferred_element_type=F32)


def _dot_f32(a, b):
    return jnp.dot(a, b, precision=lax.Precision.HIGHEST, preferred_element_type=F32)


def _keep_lanes(x, sel):
    return jnp.where(sel, x.astype(F32), 0.0).astype(x.dtype)


def _online_softmax(sc, m_prev):
    cols = [sc[:, j:j + LANES] for j in range(0, sc.shape[1], LANES)]
    m_cur = functools.reduce(jnp.maximum, cols).astype(F32)
    m_new = jnp.maximum(m_prev, jnp.max(m_cur, axis=1, keepdims=True))
    alpha = jnp.exp(m_prev - m_new)
    shift = m_new.astype(BF16)
    return m_new, alpha, jnp.concatenate([jnp.exp(c - shift) for c in cols], axis=1)


def _values_with_ones(v, n_heads):
    lead = v.shape[:-1]
    v4 = v.reshape(*lead, n_heads // 2, 2, HEAD_DIM)
    ones = jnp.ones((*lead, n_heads // 2, HEAD_DIM), v.dtype)
    va = jnp.concatenate([v4[..., 0, :], ones], axis=-1)
    vb = jnp.concatenate([ones, v4[..., 1, :]], axis=-1)
    return jnp.stack([va, vb], axis=-2).reshape(*lead, n_heads * 2 * HEAD_DIM)


def _tri_schedule(nq, kq_ratio=1):
    qs, ks = [], []
    for qi in range(nq):
        for ki in range(qi // kq_ratio + 1):
            qs.append(qi)
            ks.append(ki)
    return jnp.asarray(qs, I32), jnp.asarray(ks, I32)


def _ada_kernel(c_ref, w_ref, b_ref, o_ref):
    c = c_ref[...]
    sc = c * jax.nn.sigmoid(c)
    o_ref[...] = _dot_f32(sc, w_ref[...]) + b_ref[...]


def ada_mods(c, w, b):
    bsz, d = c.shape
    n = w.shape[1]
    tn = min(n, 1024)
    cp = jnp.zeros((8, d), F32).at[:bsz].set(c)
    out = pl.pallas_call(
        _ada_kernel,
        out_shape=jax.ShapeDtypeStruct((8, n), F32),
        grid=(n // tn,),
        in_specs=[pl.BlockSpec((8, d), lambda j: (0, 0)),
                  pl.BlockSpec((d, tn), lambda j: (0, j)),
                  pl.BlockSpec((1, tn), lambda j: (0, j))],
        out_specs=pl.BlockSpec((8, tn), lambda j: (0, j)),
        compiler_params=_cparams(("arbitrary",)),
        name="ada_mods",
    )(cp, w, b.reshape(1, n))
    return out[:bsz]


def _nmm_kernel(x_ref, g_ref, sc_ref, sh_ref, w_ref, *rest, tn, has_aux):
    if has_aux:
        wa_ref, o_ref, oa_ref = rest
    else:
        (o_ref,) = rest
    x = x_ref[...]
    ms = jnp.mean(x * x, axis=-1, keepdims=True)
    y = x * lax.rsqrt(ms + EPS) * g_ref[...]
    hf = y * (1.0 + sc_ref[0]) + sh_ref[0]
    h = hf.astype(BF16)
    n = o_ref.shape[1]
    for j0 in range(0, n, tn):
        j1 = min(j0 + tn, n)
        o_ref[:, j0:j1] = _dot(h, w_ref[:, j0:j1]).astype(o_ref.dtype)
    if has_aux:
        na = oa_ref.shape[1]
        for j0 in range(0, na, tn):
            j1 = min(j0 + tn, na)
            oa_ref[:, j0:j1] = _dot_f32(hf, wa_ref[:, j0:j1])


def norm_mod_matmul(x2, g, scale, shift, w, w_aux, seq, tm=512, tn=256):
    n_tok, din = x2.shape
    nout = w.shape[1]
    tm = min(tm, seq)
    has_aux = w_aux is not None
    bsel = lambda i: (i * tm // seq, 0, 0)
    in_specs = [pl.BlockSpec((tm, din), lambda i: (i, 0)),
                pl.BlockSpec((1, din), lambda i: (0, 0)),
                pl.BlockSpec((1, 1, din), bsel),
                pl.BlockSpec((1, 1, din), bsel),
                pl.BlockSpec((din, nout), lambda i: (0, 0))]
    out_shape = [jax.ShapeDtypeStruct((n_tok, nout), BF16)]
    out_specs = [pl.BlockSpec((tm, nout), lambda i: (i, 0))]
    args = [x2, g.reshape(1, din), scale.reshape(-1, 1, din), shift.reshape(-1, 1, din), w]
    if has_aux:
        na = w_aux.shape[1]
        in_specs.append(pl.BlockSpec((din, na), lambda i: (0, 0)))
        out_shape.append(jax.ShapeDtypeStruct((n_tok, na), F32))
        out_specs.append(pl.BlockSpec((tm, na), lambda i: (i, 0)))
        args.append(w_aux)
    outs = pl.pallas_call(
        functools.partial(_nmm_kernel, tn=min(tn, nout), has_aux=has_aux),
        out_shape=out_shape,
        grid=(n_tok // tm,),
        in_specs=in_specs,
        out_specs=out_specs,
        compiler_params=_cparams(("arbitrary",), VMEM_LIMIT_BYTES),
        name="norm_mod_matmul",
    )(*args)
    return outs if has_aux else outs[0]


def _foxcum_kernel(z_ref, fb_ref, qx_ref, kx_ref, carry):
    t = pl.program_id(1)

    @pl.when(t == 0)
    def _():
        carry[...] = jnp.zeros_like(carry)

    ts = z_ref.shape[1]
    z = z_ref[0] + fb_ref[...]
    lf = jnp.minimum(z, 0.0) - jnp.log1p(jnp.exp(-jnp.abs(z)))
    row = lax.broadcasted_iota(I32, (ts, ts), 0)
    col = lax.broadcasted_iota(I32, (ts, ts), 1)
    tri = jnp.where(row >= col, 1.0, 0.0).astype(F32)
    c = jnp.dot(tri, lf, precision=lax.Precision.HIGHEST, preferred_element_type=F32) + carry[...]
    carry[...] = c[ts - 1:ts, :]
    hi = c.astype(BF16).astype(F32)
    r1 = c - hi
    mid = r1.astype(BF16).astype(F32)
    lo = (r1 - mid).astype(BF16).astype(F32)
    lane = lax.broadcasted_iota(I32, (ts, LANES), 1)
    ph = lane % FOX_PARTS
    part = jnp.where(ph == 0, hi, jnp.where(ph == 1, mid, lo))
    first = lane < FOX_XW
    second = (lane >= FOX_XW) & (lane < 2 * FOX_XW)
    qx_ref[0] = jnp.where(first, part, jnp.where(second, 1.0, 0.0)).astype(BF16)
    kx_ref[0] = jnp.where(first, 1.0, jnp.where(second, -part, 0.0)).astype(BF16)


def fox_gate_columns(z, fb, ts=512):
    bsz, seq, _ = z.shape
    ts = min(ts, seq)
    return pl.pallas_call(
        _foxcum_kernel,
        out_shape=[jax.ShapeDtypeStruct((bsz, seq, LANES), BF16)] * 2,
        grid=(bsz, seq // ts),
        in_specs=[pl.BlockSpec((1, ts, LANES), lambda b, t: (b, t, 0)),
                  pl.BlockSpec((1, LANES), lambda b, t: (0, 0))],
        out_specs=[pl.BlockSpec((1, ts, LANES), lambda b, t: (b, t, 0))] * 2,
        scratch_shapes=[pltpu.VMEM((1, LANES), F32)],
        compiler_params=_cparams(("arbitrary", "arbitrary")),
        name="fox_gate_columns",
    )(z, fb)


def _fox_kernel(qi_ref, ki_ref, q_ref, k_ref, v_ref, qx_ref, kx_ref, o_ref, m_scr, l_scr, acc_scr, *, T):
    p = pl.program_id(1)
    s = pl.program_id(2)
    qi = qi_ref[s]
    ki = ki_ref[s]

    @pl.when(ki == 0)
    def _():
        m_scr[...] = jnp.full_like(m_scr, NEG)
        l_scr[...] = jnp.zeros_like(l_scr)
        acc_scr[...] = jnp.zeros_like(acc_scr)

    lane = lax.broadcasted_iota(I32, (1, LANES), 1)
    lo_half = lane < HEAD_DIM

    def step(masked):
        q = q_ref[0] * QK_SCALE
        qx = qx_ref[0]
        kcat = jnp.concatenate([k_ref[0], kx_ref[0]], axis=1)
        if masked:
            row = lax.broadcasted_iota(I32, (T, T), 0)
            col = lax.broadcasted_iota(I32, (T, T), 1)
            causal_add = jnp.where(col <= row, 0.0, NEG).astype(BF16)
        scs = []
        for hh in range(2):
            xl = FOX_PARTS * (2 * p + hh)
            qsel = lo_half if hh == 0 else jnp.logical_not(lo_half)
            xsel = ((lane >= xl) & (lane < xl + FOX_PARTS)) | (
                (lane >= FOX_XW + xl) & (lane < FOX_XW + xl + FOX_PARTS))
            qcat = jnp.concatenate([_keep_lanes(q, qsel), _keep_lanes(qx, xsel)], axis=1)
            sc = _dot_nt(qcat, kcat).astype(BF16)
            scs.append(sc + causal_add if masked else sc)
        pvs, alphas = [], []
        for hh in range(2):
            m_scr[hh], alpha, pm = _online_softmax(scs[hh], m_scr[hh])
            pvs.append(_dot(pm, v_ref[0, :, hh * LANES:(hh + 1) * LANES]))
            alphas.append(alpha)
        acc_scr[...] = (acc_scr[...] * jnp.where(lo_half, alphas[0], alphas[1])
                        + jnp.where(lo_half, pvs[0], pvs[1]))
        l_scr[...] = (l_scr[...] * jnp.where(lo_half, alphas[1], alphas[0])
                      + jnp.where(lo_half, pvs[1], pvs[0]))

    @pl.when(ki < qi)
    def _():
        step(False)

    @pl.when(ki == qi)
    def _():
        step(True)
        l = pltpu.roll(l_scr[...], HEAD_DIM, axis=1)
        o_ref[0] = (acc_scr[...] / l).astype(o_ref.dtype)


def fox_attention(proj, qx, kx, n_heads, T=1024):
    bsz, seq, _ = proj.shape
    T = min(T, seq)
    npair = n_heads // 2
    hw = n_heads * HEAD_DIM
    v_ones = _values_with_ones(proj[:, :, 2 * hw:3 * hw], n_heads)
    qs, ks = _tri_schedule(seq // T)
    grid_spec = pltpu.PrefetchScalarGridSpec(
        num_scalar_prefetch=2,
        grid=(bsz, npair, qs.shape[0]),
        in_specs=[
            pl.BlockSpec((1, T, LANES), lambda b, p, s, qi, ki: (b, qi[s], p)),
            pl.BlockSpec((1, T, LANES), lambda b, p, s, qi, ki: (b, ki[s], npair + p)),
            pl.BlockSpec((1, T, 2 * LANES), lambda b, p, s, qi, ki: (b, ki[s], p)),
            pl.BlockSpec((1, T, LANES), lambda b, p, s, qi, ki: (b, qi[s], 0)),
            pl.BlockSpec((1, T, LANES), lambda b, p, s, qi, ki: (b, ki[s], 0)),
        ],
        out_specs=pl.BlockSpec((1, T, LANES), lambda b, p, s, qi, ki: (b, qi[s], p)),
        scratch_shapes=[pltpu.VMEM((2, T, LANES), F32), pltpu.VMEM((T, LANES), F32),
                        pltpu.VMEM((T, LANES), F32)],
    )
    return pl.pallas_call(
        functools.partial(_fox_kernel, T=T),
        out_shape=jax.ShapeDtypeStruct((bsz, seq, npair * LANES), BF16),
        grid_spec=grid_spec,
        compiler_params=_cparams(("arbitrary", "arbitrary", "arbitrary"), VMEM_LIMIT_BYTES),
        name="fox_attention",
    )(qs, ks, proj, proj, v_ones, qx, kx)


def _dil_kernel(q_ref, kp_ref, kc_ref, vp_ref, vc_ref, bias_ref, o_ref, lse_ref, *, W, M):
    ut = pl.program_id(2)
    lane = lax.broadcasted_iota(I32, (1, LANES), 1)
    lo_half = lane < HEAD_DIM
    col = lax.broadcasted_iota(I32, (W, 2 * W), 1)
    no_prev = jnp.logical_and(ut == 0, col < W)
    for i in range(M):
        rows = slice(i * W, (i + 1) * W)
        for pp in range(2):
            cs = slice(pp * LANES, (pp + 1) * LANES)
            q = q_ref[0, rows, cs] * QK_SCALE
            if i == 0:
                k = jnp.concatenate([kp_ref[0, :, cs], kc_ref[0, rows, cs]], axis=0)
                v = jnp.concatenate([vp_ref[0, :, cs], vc_ref[0, rows, cs]], axis=0)
            else:
                k = kc_ref[0, (i - 1) * W:(i + 1) * W, cs]
                v = vc_ref[0, (i - 1) * W:(i + 1) * W, cs]
            outs, lses = [], []
            for hh in range(2):
                qsel = lo_half if hh == 0 else jnp.logical_not(lo_half)
                sc = _dot_nt(_keep_lanes(q, qsel), k) + bias_ref[pp, hh]
                if i == 0:
                    sc = jnp.where(no_prev, NEG, sc)
                mx = jnp.max(sc, axis=1, keepdims=True)
                e = jnp.exp(sc - mx)
                den = jnp.sum(e, axis=1, keepdims=True)
                outs.append(_dot((e / den).astype(BF16), v))
                lses.append(mx + jnp.log(den))
            o_ref[0, rows, cs] = jnp.where(lo_half, outs[0], outs[1])
            lse_ref[0, rows, cs] = jnp.where(lo_half, lses[0], lses[1])


def dilated_group_attention(proj, bias, g, dil, col0, W=128):
    bsz, seq, c = proj.shape
    sub = seq // dil
    pv = proj.reshape(bsz, sub, dil * c)
    gw = N_DIL_SLOTS * HEAD_DIM
    M = min(4, sub // W)
    cb = c // gw
    qb = col0 // gw
    kb = qb + 3
    vb = kb + 3
    assert c % gw == 0 and col0 % gw == 0
    prev = lambda u: jnp.maximum(u * M - 1, 0)
    out = pl.pallas_call(
        functools.partial(_dil_kernel, W=W, M=M),
        out_shape=[jax.ShapeDtypeStruct((bsz, sub, dil * gw), F32)] * 2,
        grid=(bsz, dil, sub // (M * W)),
        in_specs=[
            pl.BlockSpec((1, M * W, gw), lambda b, a, u: (b, u, a * cb + qb)),
            pl.BlockSpec((1, W, gw), lambda b, a, u: (b, prev(u), a * cb + kb)),
            pl.BlockSpec((1, M * W, gw), lambda b, a, u: (b, u, a * cb + kb)),
            pl.BlockSpec((1, W, gw), lambda b, a, u: (b, prev(u), a * cb + vb)),
            pl.BlockSpec((1, M * W, gw), lambda b, a, u: (b, u, a * cb + vb)),
            pl.BlockSpec((2, 2, W, 2 * W), lambda b, a, u: (0, 0, 0, 0)),
        ],
        out_specs=[pl.BlockSpec((1, M * W, gw), lambda b, a, u: (b, u, a))] * 2,
        compiler_params=_cparams(("arbitrary",) * 3),
        name=f"dilated_attention_g{g}",
    )(pv, pv, pv, pv, pv, bias)
    return [o.reshape(bsz * seq, 2 * LANES) for o in out]


def _t5_bucket(dist):
    max_exact = N_BUCKETS // 2
    d = jnp.maximum(dist, 0)
    df = jnp.maximum(d, 1).astype(F32)
    large = max_exact + (jnp.log(df / max_exact) / math.log(MAX_DISTANCE / max_exact)
                         * (N_BUCKETS - max_exact)).astype(I32)
    large = jnp.minimum(large, N_BUCKETS - 1)
    return jnp.where(d < max_exact, d, large)


def _bucket_bias(dist, table):
    onehot = (_t5_bucket(dist)[..., None] == jnp.arange(N_BUCKETS, dtype=I32)).astype(F32)
    out = jnp.einsum('...b,bh->h...', onehot, table.astype(F32), precision=lax.Precision.HIGHEST)
    return out


def _dil_bias_tiles(rel_bias, g, window, dil, W=128):
    assert window // dil == W
    i = np.arange(W)[:, None]
    j = np.arange(2 * W)[None, :]
    n = i + W - j
    valid = (n >= 0) & (n <= W)
    heads = rel_bias[:, g * N_DIL_SLOTS:(g + 1) * N_DIL_SLOTS]
    tiles = jnp.where(valid[None], _bucket_bias(jnp.asarray(np.clip(n, 0, W) * dil, I32), heads), NEG)
    return tiles.reshape(2, 2, W, 2 * W)


def _oproj0_kernel(fox_ref, o0_ref, o1_ref, o2_ref, l0_ref, l1_ref, l2_ref, w_ref, x_ref, gate_ref, out_ref):
    ls = [l0_ref[...], l1_ref[...], l2_ref[...]]
    os_ = [o0_ref[...], o1_ref[...], o2_ref[...]]
    mx = jnp.maximum(jnp.maximum(ls[0], ls[1]), ls[2])
    ws = [jnp.exp(l - mx) for l in ls]
    den = ws[0] + ws[1] + ws[2]
    dil = (ws[0] * os_[0] + ws[1] * os_[1] + ws[2] * os_[2]) / den
    nf = fox_ref.shape[1]
    mix = _dot(fox_ref[...], w_ref[:nf, :]) + _dot(dil.astype(BF16), w_ref[nf:, :])
    out_ref[...] = x_ref[...] + gate_ref[0] * mix


def out_proj_even(fox, dil_o, dil_l, w, x2, gate, seq, tm=512):
    n_tok, d = x2.shape
    tm = min(tm, seq)
    nf = fox.shape[1]
    nd = N_DIL_SLOTS * HEAD_DIM
    row = lambda i: (i, 0)
    return pl.pallas_call(
        _oproj0_kernel,
        out_shape=jax.ShapeDtypeStruct((n_tok, d), F32),
        grid=(n_tok // tm,),
        in_specs=[pl.BlockSpec((tm, nf), row)]
                 + [pl.BlockSpec((tm, nd), row)] * 6
                 + [pl.BlockSpec((nf + nd, d), lambda i: (0, 0)),
                    pl.BlockSpec((tm, d), row),
                    pl.BlockSpec((1, 1, d), lambda i: (i * tm // seq, 0, 0))],
        out_specs=pl.BlockSpec((tm, d), row),
        compiler_params=_cparams(("arbitrary",), VMEM_LIMIT_BYTES),
        name="out_proj_even",
    )(fox, *dil_o, *dil_l, w, x2, gate.reshape(-1, 1, d))


def _oproj_kernel(a_ref, w_ref, x_ref, gate_ref, out_ref):
    out_ref[...] = x_ref[...] + gate_ref[0] * _dot(a_ref[...], w_ref[...])


def out_proj_odd(a, w, x2, gate, seq, tm=512):
    n_tok, d = x2.shape
    tm = min(tm, seq)
    ka = a.shape[1]
    row = lambda i: (i, 0)
    return pl.pallas_call(
        _oproj_kernel,
        out_shape=jax.ShapeDtypeStruct((n_tok, d), F32),
        grid=(n_tok // tm,),
        in_specs=[pl.BlockSpec((tm, ka), row),
                  pl.BlockSpec((ka, d), lambda i: (0, 0)),
                  pl.BlockSpec((tm, d), row),
                  pl.BlockSpec((1, 1, d), lambda i: (i * tm // seq, 0, 0))],
        out_specs=pl.BlockSpec((tm, d), row),
        compiler_params=_cparams(("arbitrary",), VMEM_LIMIT_BYTES),
        name="out_proj_odd",
    )(a, w, x2, gate.reshape(-1, 1, d))


def _router_kernel(x_ref, g_ref, sc_ref, sh_ref, rw_ref, rb_ref,
                   h_ref, ti_ref, tg_ref, rk_ref, cnt_ref, carry):
    i = pl.program_id(0)

    @pl.when(i == 0)
    def _():
        carry[...] = jnp.zeros_like(carry)

    tm = x_ref.shape[0]
    x = x_ref[...]
    ms = jnp.mean(x * x, axis=-1, keepdims=True)
    y = x * lax.rsqrt(ms + EPS) * g_ref[...]
    hr = (y * (1.0 + sc_ref[0]) + sh_ref[0]).astype(BF16).astype(F32)
    bits = pltpu.bitcast(hr, I32)
    half = hr.shape[1] // 2
    h_ref[...] = lax.shift_right_logical(bits[:, :half], 16) | (bits[:, half:] & jnp.int32(-65536))
    logits = _dot_f32(hr, rw_ref[...]) + rb_ref[...]
    lane = lax.broadcasted_iota(I32, (tm, LANES), 1)
    l = logits
    vals, idxs, hots = [], [], []
    for _ in range(TOP_K):
        mx = jnp.max(l, axis=1, keepdims=True)
        idx = jnp.min(jnp.where(l == mx, lane, LANES), axis=1, keepdims=True)
        hot = lane == idx
        l = jnp.where(hot, -jnp.inf, l)
        vals.append(mx)
        idxs.append(idx)
        hots.append(hot)
    es = [jnp.exp(v - vals[0]) for v in vals]
    den = es[0] + es[1] + es[2] + es[3]
    member = jnp.zeros((tm, LANES), F32)
    for hot in hots:
        member = member + jnp.where(hot, 1.0, 0.0)
    row = lax.broadcasted_iota(I32, (tm, tm), 0)
    col = lax.broadcasted_iota(I32, (tm, tm), 1)
    tri = jnp.where(row > col, 1.0, 0.0).astype(BF16)
    before = _dot(tri, member.astype(BF16)) + carry[...]
    carry[...] = carry[...] + jnp.sum(member, axis=0, keepdims=True)
    ti = jnp.zeros((tm, LANES), I32)
    tg = jnp.zeros((tm, LANES), F32)
    rk = jnp.zeros((tm, LANES), I32)
    for k in range(TOP_K):
        rank = jnp.sum(jnp.where(hots[k], before, 0.0), axis=1, keepdims=True).astype(I32)
        ti = jnp.where(lane == k, idxs[k], ti)
        tg = jnp.where(lane == k, es[k] / den, tg)
        rk = jnp.where(lane == k, rank, rk)
    ti_ref[...] = ti
    tg_ref[...] = tg
    rk_ref[...] = rk
    cnt_ref[...] = jnp.broadcast_to(carry[...], cnt_ref.shape)


def moe_router(x2, g, scale, shift, rw, rb, seq, tm=256):
    n_tok, d = x2.shape
    tm = min(tm, seq)
    rwp = jnp.zeros((d, LANES), F32).at[:, :N_EXPERTS].set(rw)
    rbp = jnp.full((1, LANES), NEG, F32).at[0, :N_EXPERTS].set(rb)
    row = lambda i: (i, 0)
    bsel = lambda i: (i * tm // seq, 0, 0)
    return pl.pallas_call(
        _router_kernel,
        out_shape=[jax.ShapeDtypeStruct((n_tok, d // 2), I32),
                   jax.ShapeDtypeStruct((n_tok, LANES), I32),
                   jax.ShapeDtypeStruct((n_tok, LANES), F32),
                   jax.ShapeDtypeStruct((n_tok, LANES), I32),
                   jax.ShapeDtypeStruct((8, LANES), F32)],
        grid=(n_tok // tm,),
        in_specs=[pl.BlockSpec((tm, d), row),
                  pl.BlockSpec((1, d), lambda i: (0, 0)),
                  pl.BlockSpec((1, 1, d), bsel),
                  pl.BlockSpec((1, 1, d), bsel),
                  pl.BlockSpec((d, LANES), lambda i: (0, 0)),
                  pl.BlockSpec((1, LANES), lambda i: (0, 0))],
        out_specs=[pl.BlockSpec((tm, d // 2), row),
                   pl.BlockSpec((tm, LANES), row),
                   pl.BlockSpec((tm, LANES), row),
                   pl.BlockSpec((tm, LANES), row),
                   pl.BlockSpec((8, LANES), lambda i: (0, 0))],
        scratch_shapes=[pltpu.VMEM((1, LANES), F32)],
        compiler_params=_cparams(("arbitrary",)),
        name="moe_router",
    )(x2, g.reshape(1, d), scale.reshape(-1, 1, d), shift.reshape(-1, 1, d), rwp, rbp)


def _dispatch_kernel(dest_ref, h_ref, xs_in_ref, xs_ref, sem):
    del xs_in_ref
    tm = h_ref.shape[0]

    def row_copy(i, d):
        return pltpu.make_async_copy(h_ref.at[pl.ds(i, 1)], xs_ref.at[pl.ds(d, 1)], sem)

    def issue(i, c):
        for k in range(TOP_K):
            row_copy(i, dest_ref[i * TOP_K + k]).start()
        return c

    lax.fori_loop(0, tm, issue, 0)

    def drain(i, c):
        for k in range(TOP_K):
            row_copy(0, 0).wait()
        return c

    lax.fori_loop(0, tm, drain, 0)


def moe_dispatch(h32, dest_flat, rows, tm=256):
    n_tok, dw = h32.shape
    tm = min(tm, n_tok)
    xs0 = jnp.zeros((rows, dw), I32)
    return pl.pallas_call(
        _dispatch_kernel,
        out_shape=jax.ShapeDtypeStruct((rows, dw), I32),
        grid=(n_tok // tm,),
        in_specs=[pl.BlockSpec((tm * TOP_K,), lambda i: (i,), memory_space=pltpu.SMEM),
                  pl.BlockSpec((tm, dw), lambda i: (i, 0)),
                  pl.BlockSpec(memory_space=pl.ANY)],
        out_specs=pl.BlockSpec(memory_space=pl.ANY),
        scratch_shapes=[pltpu.SemaphoreType.DMA(())],
        input_output_aliases={2: 0},
        compiler_params=_cparams(("arbitrary",)),
        name="moe_dispatch",
    )(dest_flat, h32, xs0)


def _experts_kernel(ce_ref, nu_ref, x_ref, w1_ref, b1_ref, w2_ref, b2_ref, y_ref,
                    act_scr, w1_scr, w2_scr, *, tn):
    c = pl.program_id(0)
    de = act_scr.shape[1]
    active = c < nu_ref[0]

    @pl.when(jnp.logical_and(active, jnp.logical_or(c == 0, ce_ref[c] != ce_ref[jnp.maximum(c - 1, 0)])))
    def _():
        for j in range(w1_scr.shape[1] // tn):
            sl = slice(j * tn, (j + 1) * tn)
            w1_scr[:, sl] = w1_ref[0, :, sl].astype(BF16)
        for j in range(w2_scr.shape[1] // tn):
            sl = slice(j * tn, (j + 1) * tn)
            w2_scr[:, sl] = w2_ref[0, :, sl].astype(BF16)

    @pl.when(active)
    def _():
        words = x_ref[...]
        x = jnp.concatenate([pltpu.bitcast(words << 16, F32),
                             pltpu.bitcast(words & jnp.int32(-65536), F32)], axis=1).astype(BF16)
        for j in range(de // tn):
            sl = slice(j * tn, (j + 1) * tn)
            sl2 = slice(de + j * tn, de + (j + 1) * tn)
            glu = _dot(x, w1_scr[:, sl]) + b1_ref[0, :, sl]
            lin = _dot(x, w1_scr[:, sl2]) + b1_ref[0, :, sl2]
            glu = jnp.minimum(glu, SWIGLU_LIMIT)
            lin = jnp.clip(lin, -SWIGLU_LIMIT, SWIGLU_LIMIT)
            act_scr[:, sl] = (glu * jax.nn.sigmoid(SWIGLU_ALPHA * glu) * (lin + 1.0)).astype(BF16)
        a = act_scr[...]
        d = y_ref.shape[1]
        for j in range(d // tn):
            sl = slice(j * tn, (j + 1) * tn)
            y_ref[:, sl] = _dot(a, w2_scr[:, sl]) + b2_ref[0, :, sl]

    @pl.when(c >= nu_ref[0])
    def _():
        y_ref[...] = jnp.zeros_like(y_ref)


def moe_experts(xs, chunk_exp, n_used, w1, b1, w2, b2, tn=256):
    rows = xs.shape[0]
    ne, d, de2 = w1.shape
    de = de2 // 2
    n_chunks = rows // MOE_CHUNK
    grid_spec = pltpu.PrefetchScalarGridSpec(
        num_scalar_prefetch=2,
        grid=(n_chunks,),
        in_specs=[pl.BlockSpec((MOE_CHUNK, d // 2), lambda c, ce, nu: (c, 0)),
                  pl.BlockSpec((1, d, de2), lambda c, ce, nu: (ce[c], 0, 0)),
                  pl.BlockSpec((1, 1, de2), lambda c, ce, nu: (ce[c], 0, 0)),
                  pl.BlockSpec((1, de, d), lambda c, ce, nu: (ce[c], 0, 0)),
                  pl.BlockSpec((1, 1, d), lambda c, ce, nu: (ce[c], 0, 0))],
        out_specs=pl.BlockSpec((MOE_CHUNK, d), lambda c, ce, nu: (c, 0)),
        scratch_shapes=[pltpu.VMEM((MOE_CHUNK, de), BF16), pltpu.VMEM((d, de2), BF16),
                        pltpu.VMEM((de, d), BF16)],
    )
    return pl.pallas_call(
        functools.partial(_experts_kernel, tn=min(tn, de)),
        out_shape=jax.ShapeDtypeStruct((rows, d), F32),
        grid_spec=grid_spec,
        compiler_params=_cparams(("arbitrary",), VMEM_LIMIT_BYTES),
        name="moe_experts",
    )(chunk_exp, n_used, xs, w1, b1.reshape(ne, 1, de2), w2, b2.reshape(ne, 1, d))


def _combine_kernel(dest_ref, y_ref, tg_ref, x_ref, gate_ref, o_ref, buf, sem):
    tm = x_ref.shape[0]

    def row_copy(i, k, d):
        return pltpu.make_async_copy(y_ref.at[pl.ds(d, 1)], buf.at[k, pl.ds(i, 1)], sem)

    def issue(i, c):
        for k in range(TOP_K):
            row_copy(i, k, dest_ref[i * TOP_K + k]).start()
        return c

    lax.fori_loop(0, tm, issue, 0)

    def drain(i, c):
        for k in range(TOP_K):
            row_copy(0, 0, 0).wait()
        return c

    lax.fori_loop(0, tm, drain, 0)
    tg = tg_ref[...]
    moe = tg[:, 0:1] * buf[0]
    for k in range(1, TOP_K):
        moe = moe + tg[:, k:k + 1] * buf[k]
    o_ref[...] = x_ref[...] + gate_ref[0] * moe


def moe_combine(y, dest_flat, tg, x2, gate, seq, tm=256):
    n_tok, d = x2.shape
    tm = min(tm, seq)
    row = lambda i: (i, 0)
    return pl.pallas_call(
        _combine_kernel,
        out_shape=jax.ShapeDtypeStruct((n_tok, d), F32),
        grid=(n_tok // tm,),
        in_specs=[pl.BlockSpec((tm * TOP_K,), lambda i: (i,), memory_space=pltpu.SMEM),
                  pl.BlockSpec(memory_space=pl.ANY),
                  pl.BlockSpec((tm, LANES), row),
                  pl.BlockSpec((tm, d), row),
                  pl.BlockSpec((1, 1, d), lambda i: (i * tm // seq, 0, 0))],
        out_specs=pl.BlockSpec((tm, d), row),
        scratch_shapes=[pltpu.VMEM((TOP_K, tm, d), F32), pltpu.SemaphoreType.DMA(())],
        compiler_params=_cparams(("arbitrary",), VMEM_LIMIT_BYTES),
        name="moe_combine",
    )(dest_flat, y, tg, x2, gate.reshape(-1, 1, d))


def moe_block(x2, g, scale, shift, gate, rw, rb, w1, b1, w2, b2, seq):
    n_tok, d = x2.shape
    h, ti, tg, rk, cnt = moe_router(x2, g, scale, shift, rw, rb, seq)
    counts = cnt[0, :N_EXPERTS].astype(I32)
    padded = ((counts + MOE_CHUNK - 1) // MOE_CHUNK) * MOE_CHUNK
    ends_pad = jnp.cumsum(padded)
    start_pad = ends_pad - padded
    n_chunks = -(-(n_tok * TOP_K) // MOE_CHUNK) + N_EXPERTS
    rows = n_chunks * MOE_CHUNK
    experts = jnp.arange(N_EXPERTS, dtype=I32)
    slot_start = jnp.sum(jnp.where(ti[:, :TOP_K, None] == experts, start_pad, 0), axis=-1)
    dest = (slot_start + rk[:, :TOP_K]).reshape(-1)
    chunk_pos = jnp.arange(n_chunks, dtype=I32)[:, None] * MOE_CHUNK
    chunk_exp = jnp.minimum(jnp.sum((ends_pad[None, :] <= chunk_pos).astype(I32), axis=1), N_EXPERTS - 1)
    n_used = (ends_pad[-1:] // MOE_CHUNK).astype(I32)
    xs = moe_dispatch(h, dest, rows)
    y = moe_experts(xs, chunk_exp, n_used, w1, b1, w2, b2)
    return moe_combine(y, dest, tg, x2, gate, seq)


def _dsa_index_kernel(qi_ref, k2_ref, wi_ref, mask_ref, ka_scr, kb_scr, key_scr,
                      *, S, KC, n_sel, idx_scale):
    qb = pl.program_id(1)
    R = qi_ref.shape[1]
    nchunks = S // KC
    lane = lax.broadcasted_iota(I32, (1, LANES), 1)

    @pl.when(qb == 0)
    def _():
        kk = k2_ref[0]
        ka_scr[...] = jnp.where(lane < HEAD_DIM, kk, 0.0).astype(BF16)
        kb_scr[...] = jnp.where(lane >= HEAD_DIM, kk, 0.0).astype(BF16)

    nck = (qb * R + R + KC - 1) // KC
    w = wi_ref[0].astype(BF16).astype(F32)
    qv = qi_ref[0].astype(BF16)

    def relu_bf16(d):
        return jnp.maximum(d, 0.0).astype(BF16).astype(F32)

    row = qb * R + lax.broadcasted_iota(I32, (R, KC), 0)
    col0 = lax.broadcasted_iota(I32, (R, KC), 1)

    def sort_key(x):
        bits = pltpu.bitcast(x, I32)
        return jnp.where(bits < 0, bits ^ jnp.int32(0x7FFFFFFF), bits)

    def score_chunk(c, gmax):
        off = pl.multiple_of(c * KC, KC)
        ka = ka_scr[pl.ds(off, KC), :]
        kb = kb_scr[pl.ds(off, KC), :]
        sc = jnp.zeros((R, KC), F32)
        for pp in range(N_IDX_HEADS // 2):
            qp = qv[:, pp * LANES:(pp + 1) * LANES]
            sc = sc + w[:, 2 * pp:2 * pp + 1] * relu_bf16(_dot_nt(qp, ka))
            sc = sc + w[:, 2 * pp + 1:2 * pp + 2] * relu_bf16(_dot_nt(qp, kb))
        sc = sc * idx_scale
        sc = jnp.where(sc == 0.0, 0.0, sc)
        key_scr[c] = jnp.where(off + col0 <= row, sort_key(sc), INT_MIN)
        return gmax

    lax.fori_loop(0, nck, score_chunk, 0)

    def count(pred):
        def body(c, acc):
            kc = key_scr[c]
            for j in range(KC // LANES):
                acc = acc + jnp.where(pred(kc[:, j * LANES:(j + 1) * LANES], c * KC + j * LANES), 1.0, 0.0)
            return acc
        acc = lax.fori_loop(0, nck, body, jnp.zeros((R, LANES), F32))
        return jnp.broadcast_to(jnp.sum(acc, axis=1, keepdims=True), (R, LANES))

    nsel_f = float(n_sel)

    def bit_step(t, carry):
        cur, cnt_cur = carry
        cand = cur + (jnp.int32(1) << (31 - t))
        cnt = count(lambda kc, base: kc >= cand)
        ok = cnt >= nsel_f
        return jnp.where(ok, cand, cur), jnp.where(ok, cnt, cnt_cur)

    tau, cnt_tau = lax.fori_loop(
        0, 32, bit_step,
        (jnp.full((R, LANES), INT_MIN, I32), jnp.full((R, LANES), float(S), F32)))

    overflow = jnp.logical_and(tau > INT_MIN, cnt_tau > nsel_f)
    any_over = jnp.max(jnp.where(overflow, 1.0, 0.0)) > 0.0
    lane_r = lax.broadcasted_iota(I32, (R, LANES), 1)

    def write_mask(keep_fn):
        for c in range(nchunks):
            @pl.when(c < nck)
            def _():
                kc = key_scr[c]
                for j in range(KC // LANES):
                    keep = keep_fn(kc[:, j * LANES:(j + 1) * LANES], c * KC + j * LANES + lane_r)
                    mask_ref[0, :, c * KC + j * LANES:c * KC + (j + 1) * LANES] = (
                        jnp.where(keep, 1, 0).astype(jnp.int8))

            @pl.when(c >= nck)
            def _():
                mask_ref[0, :, c * KC:(c + 1) * KC] = jnp.zeros((R, KC), jnp.int8)

    @pl.when(jnp.logical_not(any_over))
    def _():
        floor = jnp.maximum(tau, INT_MIN + 1)
        write_mask(lambda kj, pos: kj >= floor)

    @pl.when(any_over)
    def _():
        need = nsel_f - count(lambda kc, base: kc > tau)

        def jstep(t, cur):
            cand = cur + (jnp.int32(1) << (S.bit_length() - 2 - t))
            cnt = count(lambda kc, base: jnp.logical_and(kc == tau, base + lane_r < cand))
            return jnp.where(cnt < need, cand, cur)

        jlim = lax.fori_loop(0, S.bit_length() - 1, jstep, jnp.zeros((R, LANES), I32))
        write_mask(lambda kj, pos: jnp.logical_and(
            jnp.logical_or(kj > tau, jnp.logical_and(kj == tau, pos <= jlim)), kj > INT_MIN))


def dsa_index_mask(aux, qcol, kcol, wcol, n_sel, R=128, KC=512):
    bsz, seq, _ = aux.shape
    R = min(R, seq)
    KC = min(KC, seq)
    nq = N_IDX_HEADS * HEAD_DIM
    return pl.pallas_call(
        functools.partial(_dsa_index_kernel, S=seq, KC=KC, n_sel=n_sel,
                          idx_scale=float(nq ** -0.5)),
        out_shape=jax.ShapeDtypeStruct((bsz, seq, seq), jnp.int8),
        grid=(bsz, seq // R),
        in_specs=[pl.BlockSpec((1, R, nq), lambda b, q: (b, q, qcol // nq)),
                  pl.BlockSpec((1, seq, LANES), lambda b, q: (b, 0, kcol // LANES)),
                  pl.BlockSpec((1, R, LANES), lambda b, q: (b, q, wcol // LANES))],
        out_specs=pl.BlockSpec((1, R, seq), lambda b, q: (b, q, 0)),
        scratch_shapes=[pltpu.VMEM((seq, LANES), BF16), pltpu.VMEM((seq, LANES), BF16),
                        pltpu.VMEM((seq // KC, R, KC), I32)],
        compiler_params=_cparams(("arbitrary", "arbitrary"), VMEM_LIMIT_BYTES),
        name="dsa_index_mask",
    )(aux, aux, aux)


def _dsa_attn_kernel(qi_ref, ki_ref, q_ref, k_ref, v_ref, mask_ref, tz_ref, o_ref,
                     m_scr, l_scr, acc_scr, *, TQ, TK, DCL):
    s = pl.program_id(1)
    qi = qi_ref[s]
    ki = ki_ref[s]
    k_last = ((qi + 1) * TQ - 1) // TK

    @pl.when(ki == 0)
    def _():
        m_scr[...] = jnp.full_like(m_scr, NEG)
        l_scr[...] = jnp.zeros_like(l_scr)
        acc_scr[...] = jnp.zeros_like(acc_scr)

    lane = lax.broadcasted_iota(I32, (1, LANES), 1)
    lo_half = lane < HEAD_DIM
    d0 = qi * TQ - ki * TK
    cb0 = (DCL - jnp.minimum(d0, DCL)) // LANES
    madd = jnp.where(mask_ref[0].astype(I32) != 0, 0.0, NEG).astype(BF16)
    npair = q_ref.shape[2] // LANES

    def logits(h):
        cs = slice((h // 2) * LANES, (h // 2 + 1) * LANES)
        qsel = lo_half if h % 2 == 0 else jnp.logical_not(lo_half)
        q = _keep_lanes(q_ref[0, :, cs] * QK_SCALE, qsel)
        bias = jnp.concatenate([tz_ref[h, cb0 + jj] for jj in range(TK // LANES)], axis=1)
        return _dot_nt(q, k_ref[0, :, cs]).astype(BF16) + bias + madd

    def weighted_values(h, pm):
        return _dot(pm, v_ref[0, :, h * LANES:(h + 1) * LANES])

    def accumulate(pp, alphas, pvs):
        cs = slice(pp * LANES, (pp + 1) * LANES)
        acc_scr[:, cs] = (acc_scr[:, cs] * jnp.where(lo_half, alphas[0], alphas[1])
                          + jnp.where(lo_half, pvs[0], pvs[1]))
        l_scr[pp] = (l_scr[pp] * jnp.where(lo_half, alphas[1], alphas[0])
                     + jnp.where(lo_half, pvs[1], pvs[0]))

    ahead = 2
    nh = 2 * npair
    queue = [logits(h) for h in range(ahead)]
    alphas, pvs, probs = {}, {}, {}
    for h in range(nh + 1):
        if h < nh:
            sc = queue.pop(0)
            if h + ahead < nh:
                queue.append(logits(h + ahead))
            m_scr[h], alphas[h], probs[h] = _online_softmax(sc, m_scr[h])
        if h >= 1:
            pvs[h - 1] = weighted_values(h - 1, probs.pop(h - 1))
            if (h - 1) % 2 == 1:
                accumulate((h - 1) // 2, (alphas[h - 2], alphas[h - 1]), (pvs[h - 2], pvs[h - 1]))

    @pl.when(ki == k_last)
    def _():
        for pp in range(npair):
            cs = slice(pp * LANES, (pp + 1) * LANES)
            l = pltpu.roll(l_scr[pp], HEAD_DIM, axis=1)
            o_ref[0, :, cs] = (acc_scr[:, cs] / l).astype(o_ref.dtype)


def _dsa_bias_table(rel_bias, seq, TQ, TK, DCL):
    i = np.arange(TQ)[:, None]
    c = np.arange(DCL + TK)[None, :]
    d = np.clip(i - c + DCL, 0, seq - 1)
    tz = _bucket_bias(jnp.asarray(d, I32), rel_bias)
    tz = tz.reshape(rel_bias.shape[1], TQ, (DCL + TK) // LANES, LANES)
    return tz.transpose(0, 2, 1, 3).astype(BF16)


def _np_t5_bucket(d):
    max_exact = N_BUCKETS // 2
    df = np.maximum(d, 1).astype(np.float32)
    large = max_exact + (np.log(df / max_exact) / math.log(MAX_DISTANCE / max_exact)
                         * (N_BUCKETS - max_exact)).astype(np.int32)
    return np.where(d < max_exact, d, np.minimum(large, N_BUCKETS - 1))


def dsa_attention(q_src, qcol, kv, mask, rel_bias, TQ=256, TK=1024):
    bsz, seq, _ = q_src.shape
    TQ = min(TQ, seq)
    TK = min(TK, seq)
    hw = N_DSA_HEADS * HEAD_DIM
    buckets = _np_t5_bucket(np.arange(seq))
    first_const = int(np.max(np.nonzero(buckets != buckets[-1])[0])) + 1
    DCL = min(seq, -(-(first_const + TK - 1) // LANES) * LANES)
    tz = _dsa_bias_table(rel_bias, seq, TQ, TK, DCL)
    qs, ks = _tri_schedule(seq // TQ, TK // TQ)
    grid_spec = pltpu.PrefetchScalarGridSpec(
        num_scalar_prefetch=2,
        grid=(bsz, qs.shape[0]),
        in_specs=[
            pl.BlockSpec((1, TQ, hw), lambda b, s, qi, ki: (b, qi[s], qcol // hw)),
            pl.BlockSpec((1, TK, hw), lambda b, s, qi, ki: (b, ki[s], 0)),
            pl.BlockSpec((1, TK, 2 * hw), lambda b, s, qi, ki: (b, ki[s], 0)),
            pl.BlockSpec((1, TQ, TK), lambda b, s, qi, ki: (b, qi[s], ki[s])),
            pl.BlockSpec(tz.shape, lambda b, s, qi, ki: (0, 0, 0, 0), pipeline_mode=pl.Buffered(1)),
        ],
        out_specs=pl.BlockSpec((1, TQ, hw), lambda b, s, qi, ki: (b, qi[s], 0)),
        scratch_shapes=[pltpu.VMEM((N_DSA_HEADS, TQ, LANES), F32),
                        pltpu.VMEM((N_DSA_HEADS // 2, TQ, LANES), F32),
                        pltpu.VMEM((TQ, hw), F32)],
    )
    return pl.pallas_call(
        functools.partial(_dsa_attn_kernel, TQ=TQ, TK=TK, DCL=DCL),
        out_shape=jax.ShapeDtypeStruct((bsz, seq, hw), BF16),
        grid_spec=grid_spec,
        compiler_params=_cparams(("arbitrary", "arbitrary"), VMEM_LIMIT_BYTES),
        name="dsa_attention",
    )(qs, ks, q_src, kv, _values_with_ones(kv[:, :, hw:], N_DSA_HEADS), mask, tz)


def _rms_kernel(x_ref, g_ref, o_ref):
    x = x_ref[...]
    ms = jnp.mean(x * x, axis=-1, keepdims=True)
    o_ref[...] = x * lax.rsqrt(ms + EPS) * g_ref[...]


def final_rmsnorm(x2, g, tm=512):
    n_tok, d = x2.shape
    tm = min(tm, n_tok)
    return pl.pallas_call(
        _rms_kernel,
        out_shape=jax.ShapeDtypeStruct((n_tok, d), F32),
        grid=(n_tok // tm,),
        in_specs=[pl.BlockSpec((tm, d), lambda i: (i, 0)), pl.BlockSpec((1, d), lambda i: (0, 0))],
        out_specs=pl.BlockSpec((tm, d), lambda i: (i, 0)),
        compiler_params=_cparams(("arbitrary",)),
        name="final_rmsnorm",
    )(x2, g.reshape(1, d))


def even_mixer_residual(x2, shift1, scale1, gate1, norm1, w_in, fox_fb, w_out, rel_bias, bsz, seq):
    d = x2.shape[1]
    nfq = 3 * N_FOX_HEADS * HEAD_DIM
    w_main = jnp.concatenate([w_in[:, :nfq], w_in[:, nfq + N_FOX_HEADS:]], axis=1).astype(BF16)
    rep = np.repeat(np.arange(N_FOX_HEADS), FOX_PARTS)
    gcols = np.concatenate([rep, rep])
    w_gate = jnp.zeros((d, LANES), F32).at[:, :2 * FOX_XW].set(w_in[:, nfq + gcols])
    fb = jnp.zeros((1, LANES), F32).at[0, :2 * FOX_XW].set(fox_fb[gcols])
    proj, gate_z = norm_mod_matmul(x2, norm1, scale1, shift1, w_main, w_gate, seq)
    c = proj.shape[1]
    proj3 = proj.reshape(bsz, seq, c)
    qx, kx = fox_gate_columns(gate_z.reshape(bsz, seq, LANES), fb)
    fox = fox_attention(proj3, qx, kx, N_FOX_HEADS)
    dil_o, dil_l = [], []
    for g, (window, dil) in enumerate(DIL_PAIRS):
        bias = _dil_bias_tiles(rel_bias, g, window, dil)
        o, l = dilated_group_attention(proj3, bias, g, dil, nfq + g * N_DIL_SLOTS * HEAD_DIM)
        dil_o.append(o)
        dil_l.append(l)
    return out_proj_even(fox.reshape(bsz * seq, -1), dil_o, dil_l, w_out.astype(BF16), x2, gate1, seq)


def odd_mixer_residual(x2, shift1, scale1, gate1, norm1, w_in, kv_norm, w_ukv, w_out, rel_bias, bsz, seq):
    d = x2.shape[1]
    nq = N_DSA_HEADS * HEAD_DIM
    ni = N_IDX_HEADS * HEAD_DIM
    c_ckv, c_qi = nq, nq + KV_RANK
    c_ki = c_qi + ni
    c_wi = c_ki + HEAD_DIM
    w_main = w_in[:, :nq].astype(BF16)
    c_w, c_k, c_q = KV_RANK, KV_RANK + LANES, KV_RANK + 2 * LANES
    w_aux = jnp.zeros((d, c_q + ni), F32)
    w_aux = w_aux.at[:, :KV_RANK].set(w_in[:, c_ckv:c_qi])
    w_aux = w_aux.at[:, c_w:c_w + N_IDX_HEADS].set(w_in[:, c_wi:])
    w_aux = w_aux.at[:, c_k:c_k + HEAD_DIM].set(w_in[:, c_ki:c_wi])
    w_aux = w_aux.at[:, c_k + HEAD_DIM:c_q].set(w_in[:, c_ki:c_wi])
    w_aux = w_aux.at[:, c_q:].set(w_in[:, c_qi:c_ki])
    proj, aux = norm_mod_matmul(x2, norm1, scale1, shift1, w_main, w_aux, seq)
    zeros = jnp.zeros((bsz, KV_RANK), F32)
    kv = norm_mod_matmul(aux[:, :KV_RANK], kv_norm, zeros, zeros, w_ukv.astype(BF16), None, seq)
    proj3 = proj.reshape(bsz, seq, -1)
    n_sel = min(DSA_TOPK, seq // 4)
    mask = dsa_index_mask(aux.reshape(bsz, seq, -1), c_q, c_k, c_w, n_sel)
    att = dsa_attention(proj3, 0, kv.reshape(bsz, seq, -1), mask, rel_bias)
    return out_proj_odd(att.reshape(bsz * seq, -1), w_out.astype(BF16), x2, gate1, seq)


def kernel(x, c, rel_bias, l0_norm1, l0_ada_w, l0_ada_b, l0_w_in, l0_fox_fb, l0_w_out, l0_norm2, l0_router_w, l0_router_b, l0_w1, l0_b1, l0_w2, l0_b2, l1_norm1, l1_ada_w, l1_ada_b, l1_w_in, l1_kv_norm, l1_w_ukv, l1_w_out, l1_norm2, l1_router_w, l1_router_b, l1_w1, l1_b1, l1_w2, l1_b2, final_norm):
    bsz, seq, d = x.shape
    x2 = x.reshape(bsz * seq, d)
    layers = (
        (l0_norm1, l0_ada_w, l0_ada_b, l0_norm2, l0_router_w, l0_router_b, l0_w1, l0_b1, l0_w2, l0_b2),
        (l1_norm1, l1_ada_w, l1_ada_b, l1_norm2, l1_router_w, l1_router_b, l1_w1, l1_b1, l1_w2, l1_b2),
    )
    for i, (norm1, ada_w, ada_b, norm2, rw, rb, w1, b1, w2, b2) in enumerate(layers):
        mods = ada_mods(c, ada_w, ada_b)
        shift1, scale1, gate1, shift2, scale2, gate2 = (mods[:, j * d:(j + 1) * d] for j in range(6))
        if i % 2 == 0:
            x2 = even_mixer_residual(x2, shift1, scale1, gate1, norm1, l0_w_in, l0_fox_fb, l0_w_out,
                                     rel_bias, bsz, seq)
        else:
            x2 = odd_mixer_residual(x2, shift1, scale1, gate1, norm1, l1_w_in, l1_kv_norm, l1_w_ukv,
                                    l1_w_out, rel_bias, bsz, seq)
        x2 = moe_block(x2, norm2, scale2, shift2, gate2, rw, rb, w1, b1, w2, b2, seq)
    return final_rmsnorm(x2, final_norm).reshape(bsz, seq, d)
```

```python
import functools
import math

import numpy as np
import jax
import jax.numpy as jnp
from jax import lax
from jax.experimental import pallas as pl
from jax.experimental.pallas import tpu as pltpu

F32, BF16, I32 = jnp.float32, jnp.bfloat16, jnp.int32

LANES = 128
VMEM_LIMIT_BYTES = 56 * 1024 * 1024

HEAD_DIM = 64
N_FOX_HEADS = 8
DIL_PAIRS = ((128, 1), (512, 4), (2048, 16))
N_DIL_SLOTS = 4
N_DSA_HEADS = 16
KV_RANK = 256
N_IDX_HEADS = 8
DSA_TOPK = 256
N_BUCKETS = 32
MAX_DISTANCE = 2048
N_EXPERTS = 32
TOP_K = 4
D_EXPERT = 1024
SWIGLU_ALPHA = 1.702
SWIGLU_LIMIT = 7.0
MOE_CHUNK = 256
EPS = 1e-6

NEG = -1e30
INT_MIN = -(2 ** 31)
QK_SCALE = HEAD_DIM ** -0.5
FOX_PARTS = 3
FOX_XW = FOX_PARTS * N_FOX_HEADS


def _cparams(sem, vmem=None):
    return pltpu.CompilerParams(dimension_semantics=sem, vmem_limit_bytes=vmem)


def _dot_nt(a, b):
    return lax.dot_general(a, b, (((1,), (1,)), ((), ())), preferred_element_type=F32)


def _dot(a, b):
    return jnp.dot(a, b, preferred_element_type=F32)


def _dot_f32(a, b):
    return jnp.dot(a, b, precision=lax.Precision.HIGHEST, preferred_element_type=F32)


def _keep_lanes(x, sel):
    return jnp.where(sel, x.astype(F32), 0.0).astype(x.dtype)


def _online_softmax(sc, m_prev):
    cols = [sc[:, j:j + LANES] for j in range(0, sc.shape[1], LANES)]
    m_cur = functools.reduce(jnp.maximum, cols).astype(F32)
    m_new = jnp.maximum(m_prev, jnp.max(m_cur, axis=1, keepdims=True))
    alpha = jnp.exp(m_prev - m_new)
    shift = m_new.astype(BF16)
    return m_new, alpha, jnp.concatenate([jnp.exp(c - shift) for c in cols], axis=1)


def _values_with_ones(v, n_heads):
    lead = v.shape[:-1]
    v4 = v.reshape(*lead, n_heads // 2, 2, HEAD_DIM)
    ones = jnp.ones((*lead, n_heads // 2, HEAD_DIM), v.dtype)
    va = jnp.concatenate([v4[..., 0, :], ones], axis=-1)
    vb = jnp.concatenate([ones, v4[..., 1, :]], axis=-1)
    return jnp.stack([va, vb], axis=-2).reshape(*lead, n_heads * 2 * HEAD_DIM)


def _tri_schedule(nq, kq_ratio=1):
    qs, ks = [], []
    for qi in range(nq):
        for ki in range(qi // kq_ratio + 1):
            qs.append(qi)
            ks.append(ki)
    return jnp.asarray(qs, I32), jnp.asarray(ks, I32)


def _ada_kernel(c_ref, w_ref, b_ref, o_ref):
    c = c_ref[...]
    sc = c * jax.nn.sigmoid(c)
    o_ref[...] = _dot_f32(sc, w_ref[...]) + b_ref[...]


def ada_mods(c, w, b):
    bsz, d = c.shape
    n = w.shape[1]
    tn = min(n, 1024)
    cp = jnp.zeros((8, d), F32).at[:bsz].set(c)
    out = pl.pallas_call(
        _ada_kernel,
        out_shape=jax.ShapeDtypeStruct((8, n), F32),
        grid=(n // tn,),
        in_specs=[pl.BlockSpec((8, d), lambda j: (0, 0)),
                  pl.BlockSpec((d, tn), lambda j: (0, j)),
                  pl.BlockSpec((1, tn), lambda j: (0, j))],
        out_specs=pl.BlockSpec((8, tn), lambda j: (0, j)),
        compiler_params=_cparams(("arbitrary",)),
        name="ada_mods",
    )(cp, w, b.reshape(1, n))
    return out[:bsz]


def _nmm_kernel(x_ref, g_ref, sc_ref, sh_ref, w_ref, *rest, tn, has_aux):
    if has_aux:
        wa_ref, o_ref, oa_ref = rest
    else:
        (o_ref,) = rest
    x = x_ref[...]
    ms = jnp.mean(x * x, axis=-1, keepdims=True)
    y = x * lax.rsqrt(ms + EPS) * g_ref[...]
    hf = y * (1.0 + sc_ref[0]) + sh_ref[0]
    h = hf.astype(BF16)
    n = o_ref.shape[1]
    for j0 in range(0, n, tn):
        j1 = min(j0 + tn, n)
        o_ref[:, j0:j1] = _dot(h, w_ref[:, j0:j1]).astype(o_ref.dtype)
    if has_aux:
        na = oa_ref.shape[1]
        for j0 in range(0, na, tn):
            j1 = min(j0 + tn, na)
            oa_ref[:, j0:j1] = _dot_f32(hf, wa_ref[:, j0:j1])


def norm_mod_matmul(x2, g, scale, shift, w, w_aux, seq, tm=512, tn=256):
    n_tok, din = x2.shape
    nout = w.shape[1]
    tm = min(tm, seq)
    has_aux = w_aux is not None
    bsel = lambda i: (i * tm // seq, 0, 0)
    in_specs = [pl.BlockSpec((tm, din), lambda i: (i, 0)),
                pl.BlockSpec((1, din), lambda i: (0, 0)),
                pl.BlockSpec((1, 1, din), bsel),
                pl.BlockSpec((1, 1, din), bsel),
                pl.BlockSpec((din, nout), lambda i: (0, 0))]
    out_shape = [jax.ShapeDtypeStruct((n_tok, nout), BF16)]
    out_specs = [pl.BlockSpec((tm, nout), lambda i: (i, 0))]
    args = [x2, g.reshape(1, din), scale.reshape(-1, 1, din), shift.reshape(-1, 1, din), w]
    if has_aux:
        na = w_aux.shape[1]
        in_specs.append(pl.BlockSpec((din, na), lambda i: (0, 0)))
        out_shape.append(jax.ShapeDtypeStruct((n_tok, na), F32))
        out_specs.append(pl.BlockSpec((tm, na), lambda i: (i, 0)))
        args.append(w_aux)
    outs = pl.pallas_call(
        functools.partial(_nmm_kernel, tn=min(tn, nout), has_aux=has_aux),
        out_shape=out_shape,
        grid=(n_tok // tm,),
        in_specs=in_specs,
        out_specs=out_specs,
        compiler_params=_cparams(("arbitrary",), VMEM_LIMIT_BYTES),
        name="norm_mod_matmul",
    )(*args)
    return outs if has_aux else outs[0]


def _foxcum_kernel(z_ref, fb_ref, qx_ref, kx_ref, carry):
    t = pl.program_id(1)

    @pl.when(t == 0)
    def _():
        carry[...] = jnp.zeros_like(carry)

    ts = z_ref.shape[1]
    z = z_ref[0] + fb_ref[...]
    lf = jnp.minimum(z, 0.0) - jnp.log1p(jnp.exp(-jnp.abs(z)))
    row = lax.broadcasted_iota(I32, (ts, ts), 0)
    col = lax.broadcasted_iota(I32, (ts, ts), 1)
    tri = jnp.where(row >= col, 1.0, 0.0).astype(F32)
    c = jnp.dot(tri, lf, precision=lax.Precision.HIGHEST, preferred_element_type=F32) + carry[...]
    carry[...] = c[ts - 1:ts, :]
    hi = c.astype(BF16).astype(F32)
    r1 = c - hi
    mid = r1.astype(BF16).astype(F32)
    lo = (r1 - mid).astype(BF16).astype(F32)
    lane = lax.broadcasted_iota(I32, (ts, LANES), 1)
    ph = lane % FOX_PARTS
    part = jnp.where(ph == 0, hi, jnp.where(ph == 1, mid, lo))
    first = lane < FOX_XW
    second = (lane >= FOX_XW) & (lane < 2 * FOX_XW)
    qx_ref[0] = jnp.where(first, part, jnp.where(second, 1.0, 0.0)).astype(BF16)
    kx_ref[0] = jnp.where(first, 1.0, jnp.where(second, -part, 0.0)).astype(BF16)


def fox_gate_columns(z, fb, ts=512):
    bsz, seq, _ = z.shape
    ts = min(ts, seq)
    return pl.pallas_call(
        _foxcum_kernel,
        out_shape=[jax.ShapeDtypeStruct((bsz, seq, LANES), BF16)] * 2,
        grid=(bsz, seq // ts),
        in_specs=[pl.BlockSpec((1, ts, LANES), lambda b, t: (b, t, 0)),
                  pl.BlockSpec((1, LANES), lambda b, t: (0, 0))],
        out_specs=[pl.BlockSpec((1, ts, LANES), lambda b, t: (b, t, 0))] * 2,
        scratch_shapes=[pltpu.VMEM((1, LANES), F32)],
        compiler_params=_cparams(("arbitrary", "arbitrary")),
        name="fox_gate_columns",
    )(z, fb)


def _fox_kernel(qi_ref, ki_ref, q_ref, k_ref, v_ref, qx_ref, kx_ref, o_ref, m_scr, l_scr, acc_scr, *, T):
    p = pl.program_id(1)
    s = pl.program_id(2)
    qi = qi_ref[s]
    ki = ki_ref[s]

    @pl.when(ki == 0)
    def _():
        m_scr[...] = jnp.full_like(m_scr, NEG)
        l_scr[...] = jnp.zeros_like(l_scr)
        acc_scr[...] = jnp.zeros_like(acc_scr)

    lane = lax.broadcasted_iota(I32, (1, LANES), 1)
    lo_half = lane < HEAD_DIM

    def step(masked):
        q = q_ref[0] * QK_SCALE
        qx = qx_ref[0]
        kcat = jnp.concatenate([k_ref[0], kx_ref[0]], axis=1)
        if masked:
            row = lax.broadcasted_iota(I32, (T, T), 0)
            col = lax.broadcasted_iota(I32, (T, T), 1)
            causal_add = jnp.where(col <= row, 0.0, NEG).astype(BF16)
        scs = []
        for hh in range(2):
            xl = FOX_PARTS * (2 * p + hh)
            qsel = lo_half if hh == 0 else jnp.logical_not(lo_half)
            xsel = ((lane >= xl) & (lane < xl + FOX_PARTS)) | (
                (lane >= FOX_XW + xl) & (lane < FOX_XW + xl + FOX_PARTS))
            qcat = jnp.concatenate([_keep_lanes(q, qsel), _keep_lanes(qx, xsel)], axis=1)
            sc = _dot_nt(qcat, kcat).astype(BF16)
            scs.append(sc + causal_add if masked else sc)
        pvs, alphas = [], []
        for hh in range(2):
            m_scr[hh], alpha, pm = _online_softmax(scs[hh], m_scr[hh])
            pvs.append(_dot(pm, v_ref[0, :, hh * LANES:(hh + 1) * LANES]))
            alphas.append(alpha)
        acc_scr[...] = (acc_scr[...] * jnp.where(lo_half, alphas[0], alphas[1])
                        + jnp.where(lo_half, pvs[0], pvs[1]))
        l_scr[...] = (l_scr[...] * jnp.where(lo_half, alphas[1], alphas[0])
                      + jnp.where(lo_half, pvs[1], pvs[0]))

    @pl.when(ki < qi)
    def _():
        step(False)

    @pl.when(ki == qi)
    def _():
        step(True)
        l = pltpu.roll(l_scr[...], HEAD_DIM, axis=1)
        o_ref[0] = (acc_scr[...] / l).astype(o_ref.dtype)


def fox_attention(proj, qx, kx, n_heads, T=1024):
    bsz, seq, _ = proj.shape
    T = min(T, seq)
    npair = n_heads // 2
    hw = n_heads * HEAD_DIM
    v_ones = _values_with_ones(proj[:, :, 2 * hw:3 * hw], n_heads)
    qs, ks = _tri_schedule(seq // T)
    grid_spec = pltpu.PrefetchScalarGridSpec(
        num_scalar_prefetch=2,
        grid=(bsz, npair, qs.shape[0]),
        in_specs=[
            pl.BlockSpec((1, T, LANES), lambda b, p, s, qi, ki: (b, qi[s], p)),
            pl.BlockSpec((1, T, LANES), lambda b, p, s, qi, ki: (b, ki[s], npair + p)),
            pl.BlockSpec((1, T, 2 * LANES), lambda b, p, s, qi, ki: (b, ki[s], p)),
            pl.BlockSpec((1, T, LANES), lambda b, p, s, qi, ki: (b, qi[s], 0)),
            pl.BlockSpec((1, T, LANES), lambda b, p, s, qi, ki: (b, ki[s], 0)),
        ],
        out_specs=pl.BlockSpec((1, T, LANES), lambda b, p, s, qi, ki: (b, qi[s], p)),
        scratch_shapes=[pltpu.VMEM((2, T, LANES), F32), pltpu.VMEM((T, LANES), F32),
                        pltpu.VMEM((T, LANES), F32)],
    )
    return pl.pallas_call(
        functools.partial(_fox_kernel, T=T),
        out_shape=jax.ShapeDtypeStruct((bsz, seq, npair * LANES), BF16),
        grid_spec=grid_spec,
        compiler_params=_cparams(("arbitrary", "arbitrary", "arbitrary"), VMEM_LIMIT_BYTES),
        name="fox_attention",
    )(qs, ks, proj, proj, v_ones, qx, kx)


def _dil_kernel(q_ref, kp_ref, kc_ref, vp_ref, vc_ref, bias_ref, o_ref, lse_ref, *, W, M):
    ut = pl.program_id(2)
    lane = lax.broadcasted_iota(I32, (1, LANES), 1)
    lo_half = lane < HEAD_DIM
    col = lax.broadcasted_iota(I32, (W, 2 * W), 1)
    no_prev = jnp.logical_and(ut == 0, col < W)
    for i in range(M):
        rows = slice(i * W, (i + 1) * W)
        for pp in range(2):
            cs = slice(pp * LANES, (pp + 1) * LANES)
            q = q_ref[0, rows, cs] * QK_SCALE
            if i == 0:
                k = jnp.concatenate([kp_ref[0, :, cs], kc_ref[0, rows, cs]], axis=0)
                v = jnp.concatenate([vp_ref[0, :, cs], vc_ref[0, rows, cs]], axis=0)
            else:
                k = kc_ref[0, (i - 1) * W:(i + 1) * W, cs]
                v = vc_ref[0, (i - 1) * W:(i + 1) * W, cs]
            outs, lses = [], []
            for hh in range(2):
                qsel = lo_half if hh == 0 else jnp.logical_not(lo_half)
                sc = _dot_nt(_keep_lanes(q, qsel), k) + bias_ref[pp, hh]
                if i == 0:
                    sc = jnp.where(no_prev, NEG, sc)
                mx = jnp.max(sc, axis=1, keepdims=True)
                e = jnp.exp(sc - mx)
                den = jnp.sum(e, axis=1, keepdims=True)
                outs.append(_dot((e / den).astype(BF16), v))
                lses.append(mx + jnp.log(den))
            o_ref[0, rows, cs] = jnp.where(lo_half, outs[0], outs[1])
            lse_ref[0, rows, cs] = jnp.where(lo_half, lses[0], lses[1])


def dilated_group_attention(proj, bias, g, dil, col0, W=128):
    bsz, seq, c = proj.shape
    sub = seq // dil
    pv = proj.reshape(bsz, sub, dil * c)
    gw = N_DIL_SLOTS * HEAD_DIM
    M = min(4, sub // W)
    cb = c // gw
    qb = col0 // gw
    kb = qb + 3
    vb = kb + 3
    assert c % gw == 0 and col0 % gw == 0
    prev = lambda u: jnp.maximum(u * M - 1, 0)
    out = pl.pallas_call(
        functools.partial(_dil_kernel, W=W, M=M),
        out_shape=[jax.ShapeDtypeStruct((bsz, sub, dil * gw), F32)] * 2,
        grid=(bsz, dil, sub // (M * W)),
        in_specs=[
            pl.BlockSpec((1, M * W, gw), lambda b, a, u: (b, u, a * cb + qb)),
            pl.BlockSpec((1, W, gw), lambda b, a, u: (b, prev(u), a * cb + kb)),
            pl.BlockSpec((1, M * W, gw), lambda b, a, u: (b, u, a * cb + kb)),
            pl.BlockSpec((1, W, gw), lambda b, a, u: (b, prev(u), a * cb + vb)),
            pl.BlockSpec((1, M * W, gw), lambda b, a, u: (b, u, a * cb + vb)),
            pl.BlockSpec((2, 2, W, 2 * W), lambda b, a, u: (0, 0, 0, 0)),
        ],
        out_specs=[pl.BlockSpec((1, M * W, gw), lambda b, a, u: (b, u, a))] * 2,
        compiler_params=_cparams(("arbitrary",) * 3),
        name=f"dilated_attention_g{g}",
    )(pv, pv, pv, pv, pv, bias)
    return [o.reshape(bsz * seq, 2 * LANES) for o in out]


def _t5_bucket(dist):
    max_exact = N_BUCKETS // 2
    d = jnp.maximum(dist, 0)
    df = jnp.maximum(d, 1).astype(F32)
    large = max_exact + (jnp.log(df / max_exact) / math.log(MAX_DISTANCE / max_exact)
                         * (N_BUCKETS - max_exact)).astype(I32)
    large = jnp.minimum(large, N_BUCKETS - 1)
    return jnp.where(d < max_exact, d, large)


def _bucket_bias(dist, table):
    onehot = (_t5_bucket(dist)[..., None] == jnp.arange(N_BUCKETS, dtype=I32)).astype(F32)
    out = jnp.einsum('...b,bh->h...', onehot, table.astype(F32), precision=lax.Precision.HIGHEST)
    return out


def _dil_bias_tiles(rel_bias, g, window, dil, W=128):
    assert window // dil == W
    i = np.arange(W)[:, None]
    j = np.arange(2 * W)[None, :]
    n = i + W - j
    valid = (n >= 0) & (n <= W)
    heads = rel_bias[:, g * N_DIL_SLOTS:(g + 1) * N_DIL_SLOTS]
    tiles = jnp.where(valid[None], _bucket_bias(jnp.asarray(np.clip(n, 0, W) * dil, I32), heads), NEG)
    return tiles.reshape(2, 2, W, 2 * W)


def _oproj0_kernel(fox_ref, o0_ref, o1_ref, o2_ref, l0_ref, l1_ref, l2_ref, w_ref, x_ref, gate_ref, out_ref):
    ls = [l0_ref[...], l1_ref[...], l2_ref[...]]
    os_ = [o0_ref[...], o1_ref[...], o2_ref[...]]
    mx = jnp.maximum(jnp.maximum(ls[0], ls[1]), ls[2])
    ws = [jnp.exp(l - mx) for l in ls]
    den = ws[0] + ws[1] + ws[2]
    dil = (ws[0] * os_[0] + ws[1] * os_[1] + ws[2] * os_[2]) / den
    nf = fox_ref.shape[1]
    mix = _dot(fox_ref[...], w_ref[:nf, :]) + _dot(dil.astype(BF16), w_ref[nf:, :])
    out_ref[...] = x_ref[...] + gate_ref[0] * mix


def out_proj_even(fox, dil_o, dil_l, w, x2, gate, seq, tm=512):
    n_tok, d = x2.shape
    tm = min(tm, seq)
    nf = fox.shape[1]
    nd = N_DIL_SLOTS * HEAD_DIM
    row = lambda i: (i, 0)
    return pl.pallas_call(
        _oproj0_kernel,
        out_shape=jax.ShapeDtypeStruct((n_tok, d), F32),
        grid=(n_tok // tm,),
        in_specs=[pl.BlockSpec((tm, nf), row)]
                 + [pl.BlockSpec((tm, nd), row)] * 6
                 + [pl.BlockSpec((nf + nd, d), lambda i: (0, 0)),
                    pl.BlockSpec((tm, d), row),
                    pl.BlockSpec((1, 1, d), lambda i: (i * tm // seq, 0, 0))],
        out_specs=pl.BlockSpec((tm, d), row),
        compiler_params=_cparams(("arbitrary",), VMEM_LIMIT_BYTES),
        name="out_proj_even",
    )(fox, *dil_o, *dil_l, w, x2, gate.reshape(-1, 1, d))


def _oproj_kernel(a_ref, w_ref, x_ref, gate_ref, out_ref):
    out_ref[...] = x_ref[...] + gate_ref[0] * _dot(a_ref[...], w_ref[...])


def out_proj_odd(a, w, x2, gate, seq, tm=512):
    n_tok, d = x2.shape
    tm = min(tm, seq)
    ka = a.shape[1]
    row = lambda i: (i, 0)
    return pl.pallas_call(
        _oproj_kernel,
        out_shape=jax.ShapeDtypeStruct((n_tok, d), F32),
        grid=(n_tok // tm,),
        in_specs=[pl.BlockSpec((tm, ka), row),
                  pl.BlockSpec((ka, d), lambda i: (0, 0)),
                  pl.BlockSpec((tm, d), row),
                  pl.BlockSpec((1, 1, d), lambda i: (i * tm // seq, 0, 0))],
        out_specs=pl.BlockSpec((tm, d), row),
        compiler_params=_cparams(("arbitrary",), VMEM_LIMIT_BYTES),
        name="out_proj_odd",
    )(a, w, x2, gate.reshape(-1, 1, d))


def _router_kernel(x_ref, g_ref, sc_ref, sh_ref, rw_ref, rb_ref,
                   h_ref, ti_ref, tg_ref, rk_ref, cnt_ref, carry):
    i = pl.program_id(0)

    @pl.when(i == 0)
    def _():
        carry[...] = jnp.zeros_like(carry)

    tm = x_ref.shape[0]
    x = x_ref[...]
    ms = jnp.mean(x * x, axis=-1, keepdims=True)
    y = x * lax.rsqrt(ms + EPS) * g_ref[...]
    hr = (y * (1.0 + sc_ref[0]) + sh_ref[0]).astype(BF16).astype(F32)
    bits = pltpu.bitcast(hr, I32)
    half = hr.shape[1] // 2
    h_ref[...] = lax.shift_right_logical(bits[:, :half], 16) | (bits[:, half:] & jnp.int32(-65536))
    logits = _dot_f32(hr, rw_ref[...]) + rb_ref[...]
    lane = lax.broadcasted_iota(I32, (tm, LANES), 1)
    l = logits
    vals, idxs, hots = [], [], []
    for _ in range(TOP_K):
        mx = jnp.max(l, axis=1, keepdims=True)
        idx = jnp.min(jnp.where(l == mx, lane, LANES), axis=1, keepdims=True)
        hot = lane == idx
        l = jnp.where(hot, -jnp.inf, l)
        vals.append(mx)
        idxs.append(idx)
        hots.append(hot)
    es = [jnp.exp(v - vals[0]) for v in vals]
    den = es[0] + es[1] + es[2] + es[3]
    member = jnp.zeros((tm, LANES), F32)
    for hot in hots:
        member = member + jnp.where(hot, 1.0, 0.0)
    row = lax.broadcasted_iota(I32, (tm, tm), 0)
    col = lax.broadcasted_iota(I32, (tm, tm), 1)
    tri = jnp.where(row > col, 1.0, 0.0).astype(BF16)
    before = _dot(tri, member.astype(BF16)) + carry[...]
    carry[...] = carry[...] + jnp.sum(member, axis=0, keepdims=True)
    ti = jnp.zeros((tm, LANES), I32)
    tg = jnp.zeros((tm, LANES), F32)
    rk = jnp.zeros((tm, LANES), I32)
    for k in range(TOP_K):
        rank = jnp.sum(jnp.where(hots[k], before, 0.0), axis=1, keepdims=True).astype(I32)
        ti = jnp.where(lane == k, idxs[k], ti)
        tg = jnp.where(lane == k, es[k] / den, tg)
        rk = jnp.where(lane == k, rank, rk)
    ti_ref[...] = ti
    tg_ref[...] = tg
    rk_ref[...] = rk
    cnt_ref[...] = jnp.broadcast_to(carry[...], cnt_ref.shape)


def moe_router(x2, g, scale, shift, rw, rb, seq, tm=256):
    n_tok, d = x2.shape
    tm = min(tm, seq)
    rwp = jnp.zeros((d, LANES), F32).at[:, :N_EXPERTS].set(rw)
    rbp = jnp.full((1, LANES), NEG, F32).at[0, :N_EXPERTS].set(rb)
    row = lambda i: (i, 0)
    bsel = lambda i: (i * tm // seq, 0, 0)
    return pl.pallas_call(
        _router_kernel,
        out_shape=[jax.ShapeDtypeStruct((n_tok, d // 2), I32),
                   jax.ShapeDtypeStruct((n_tok, LANES), I32),
                   jax.ShapeDtypeStruct((n_tok, LANES), F32),
                   jax.ShapeDtypeStruct((n_tok, LANES), I32),
                   jax.ShapeDtypeStruct((8, LANES), F32)],
        grid=(n_tok // tm,),
        in_specs=[pl.BlockSpec((tm, d), row),
                  pl.BlockSpec((1, d), lambda i: (0, 0)),
                  pl.BlockSpec((1, 1, d), bsel),
                  pl.BlockSpec((1, 1, d), bsel),
                  pl.BlockSpec((d, LANES), lambda i: (0, 0)),
                  pl.BlockSpec((1, LANES), lambda i: (0, 0))],
        out_specs=[pl.BlockSpec((tm, d // 2), row),
                   pl.BlockSpec((tm, LANES), row),
                   pl.BlockSpec((tm, LANES), row),
                   pl.BlockSpec((tm, LANES), row),
                   pl.BlockSpec((8, LANES), lambda i: (0, 0))],
        scratch_shapes=[pltpu.VMEM((1, LANES), F32)],
        compiler_params=_cparams(("arbitrary",)),
        name="moe_router",
    )(x2, g.reshape(1, d), scale.reshape(-1, 1, d), shift.reshape(-1, 1, d), rwp, rbp)


def _dispatch_kernel(dest_ref, h_ref, xs_in_ref, xs_ref, sem):
    del xs_in_ref
    tm = h_ref.shape[0]

    def row_copy(i, d):
        return pltpu.make_async_copy(h_ref.at[pl.ds(i, 1)], xs_ref.at[pl.ds(d, 1)], sem)

    def issue(i, c):
        for k in range(TOP_K):
            row_copy(i, dest_ref[i * TOP_K + k]).start(priority=k % 2)
        return c

    lax.fori_loop(0, tm, issue, 0)

    def drain(i, c):
        for k in range(TOP_K):
            row_copy(0, 0).wait()
        return c

    lax.fori_loop(0, tm, drain, 0)


def moe_dispatch(h32, dest_flat, rows, tm=256):
    n_tok, dw = h32.shape
    tm = min(tm, n_tok)
    xs0 = jnp.zeros((rows, dw), I32)
    return pl.pallas_call(
        _dispatch_kernel,
        out_shape=jax.ShapeDtypeStruct((rows, dw), I32),
        grid=(n_tok // tm,),
        in_specs=[pl.BlockSpec((tm * TOP_K,), lambda i: (i,), memory_space=pltpu.SMEM),
                  pl.BlockSpec((tm, dw), lambda i: (i, 0)),
                  pl.BlockSpec(memory_space=pl.ANY)],
        out_specs=pl.BlockSpec(memory_space=pl.ANY),
        scratch_shapes=[pltpu.SemaphoreType.DMA(())],
        input_output_aliases={2: 0},
        compiler_params=_cparams(("arbitrary",)),
        name="moe_dispatch",
    )(dest_flat, h32, xs0)


def _experts_kernel(ce_ref, nu_ref, x_ref, w1_ref, b1_ref, w2_ref, b2_ref, y_ref,
                    act_scr, w1_scr, w2_scr, *, tn):
    c = pl.program_id(0)
    de = act_scr.shape[1]
    active = c < nu_ref[0]

    @pl.when(jnp.logical_and(active, jnp.logical_or(c == 0, ce_ref[c] != ce_ref[jnp.maximum(c - 1, 0)])))
    def _():
        for j in range(w1_scr.shape[1] // tn):
            sl = slice(j * tn, (j + 1) * tn)
            w1_scr[:, sl] = w1_ref[0, :, sl].astype(BF16)
        for j in range(w2_scr.shape[1] // tn):
            sl = slice(j * tn, (j + 1) * tn)
            w2_scr[:, sl] = w2_ref[0, :, sl].astype(BF16)

    @pl.when(active)
    def _():
        words = x_ref[...]
        x = jnp.concatenate([pltpu.bitcast(words << 16, F32),
                             pltpu.bitcast(words & jnp.int32(-65536), F32)], axis=1).astype(BF16)
        for j in range(de // tn):
            sl = slice(j * tn, (j + 1) * tn)
            sl2 = slice(de + j * tn, de + (j + 1) * tn)
            glu = _dot(x, w1_scr[:, sl]) + b1_ref[0, :, sl]
            lin = _dot(x, w1_scr[:, sl2]) + b1_ref[0, :, sl2]
            glu = jnp.minimum(glu, SWIGLU_LIMIT)
            lin = jnp.clip(lin, -SWIGLU_LIMIT, SWIGLU_LIMIT)
            act_scr[:, sl] = (glu * jax.nn.sigmoid(SWIGLU_ALPHA * glu) * (lin + 1.0)).astype(BF16)
        a = act_scr[...]
        d = y_ref.shape[1]
        for j in range(d // tn):
            sl = slice(j * tn, (j + 1) * tn)
            y_ref[:, sl] = _dot(a, w2_scr[:, sl]) + b2_ref[0, :, sl]

    @pl.when(c >= nu_ref[0])
    def _():
        y_ref[...] = jnp.zeros_like(y_ref)


def moe_experts(xs, chunk_exp, n_used, w1, b1, w2, b2, tn=256):
    rows = xs.shape[0]
    ne, d, de2 = w1.shape
    de = de2 // 2
    n_chunks = rows // MOE_CHUNK
    grid_spec = pltpu.PrefetchScalarGridSpec(
        num_scalar_prefetch=2,
        grid=(n_chunks,),
        in_specs=[pl.BlockSpec((MOE_CHUNK, d // 2), lambda c, ce, nu: (c, 0)),
                  pl.BlockSpec((1, d, de2), lambda c, ce, nu: (ce[c], 0, 0)),
                  pl.BlockSpec((1, 1, de2), lambda c, ce, nu: (ce[c], 0, 0)),
                  pl.BlockSpec((1, de, d), lambda c, ce, nu: (ce[c], 0, 0)),
                  pl.BlockSpec((1, 1, d), lambda c, ce, nu: (ce[c], 0, 0))],
        out_specs=pl.BlockSpec((MOE_CHUNK, d), lambda c, ce, nu: (c, 0)),
        scratch_shapes=[pltpu.VMEM((MOE_CHUNK, de), BF16), pltpu.VMEM((d, de2), BF16),
                        pltpu.VMEM((de, d), BF16)],
    )
    return pl.pallas_call(
        functools.partial(_experts_kernel, tn=min(tn, de)),
        out_shape=jax.ShapeDtypeStruct((rows, d), F32),
        grid_spec=grid_spec,
        compiler_params=_cparams(("arbitrary",), VMEM_LIMIT_BYTES),
        name="moe_experts",
    )(chunk_exp, n_used, xs, w1, b1.reshape(ne, 1, de2), w2, b2.reshape(ne, 1, d))


def _combine_kernel(dest_ref, y_ref, tg_ref, x_ref, gate_ref, o_ref, buf, sem):
    tm = x_ref.shape[0]

    def row_copy(i, k, d):
        return pltpu.make_async_copy(y_ref.at[pl.ds(d, 1)], buf.at[k, pl.ds(i, 1)], sem)

    def issue(i, c):
        for k in range(TOP_K):
            row_copy(i, k, dest_ref[i * TOP_K + k]).start(priority=k % 2)
        return c

    lax.fori_loop(0, tm, issue, 0)

    def drain(i, c):
        for k in range(TOP_K):
            row_copy(0, 0, 0).wait()
        return c

    lax.fori_loop(0, tm, drain, 0)
    tg = tg_ref[...]
    moe = tg[:, 0:1] * buf[0]
    for k in range(1, TOP_K):
        moe = moe + tg[:, k:k + 1] * buf[k]
    o_ref[...] = x_ref[...] + gate_ref[0] * moe


def moe_combine(y, dest_flat, tg, x2, gate, seq, tm=256):
    n_tok, d = x2.shape
    tm = min(tm, seq)
    row = lambda i: (i, 0)
    return pl.pallas_call(
        _combine_kernel,
        out_shape=jax.ShapeDtypeStruct((n_tok, d), F32),
        grid=(n_tok // tm,),
        in_specs=[pl.BlockSpec((tm * TOP_K,), lambda i: (i,), memory_space=pltpu.SMEM),
                  pl.BlockSpec(memory_space=pl.ANY),
                  pl.BlockSpec((tm, LANES), row),
                  pl.BlockSpec((tm, d), row),
                  pl.BlockSpec((1, 1, d), lambda i: (i * tm // seq, 0, 0))],
        out_specs=pl.BlockSpec((tm, d), row),
        scratch_shapes=[pltpu.VMEM((TOP_K, tm, d), F32), pltpu.SemaphoreType.DMA(())],
        compiler_params=_cparams(("arbitrary",), VMEM_LIMIT_BYTES),
        name="moe_combine",
    )(dest_flat, y, tg, x2, gate.reshape(-1, 1, d))


def moe_block(x2, g, scale, shift, gate, rw, rb, w1, b1, w2, b2, seq):
    n_tok, d = x2.shape
    h, ti, tg, rk, cnt = moe_router(x2, g, scale, shift, rw, rb, seq)
    counts = cnt[0, :N_EXPERTS].astype(I32)
    padded = ((counts + MOE_CHUNK - 1) // MOE_CHUNK) * MOE_CHUNK
    ends_pad = jnp.cumsum(padded)
    start_pad = ends_pad - padded
    n_chunks = -(-(n_tok * TOP_K) // MOE_CHUNK) + N_EXPERTS
    rows = n_chunks * MOE_CHUNK
    experts = jnp.arange(N_EXPERTS, dtype=I32)
    slot_start = jnp.sum(jnp.where(ti[:, :TOP_K, None] == experts, start_pad, 0), axis=-1)
    dest = (slot_start + rk[:, :TOP_K]).reshape(-1)
    chunk_pos = jnp.arange(n_chunks, dtype=I32)[:, None] * MOE_CHUNK
    chunk_exp = jnp.minimum(jnp.sum((ends_pad[None, :] <= chunk_pos).astype(I32), axis=1), N_EXPERTS - 1)
    n_used = (ends_pad[-1:] // MOE_CHUNK).astype(I32)
    xs = moe_dispatch(h, dest, rows)
    y = moe_experts(xs, chunk_exp, n_used, w1, b1, w2, b2)
    return moe_combine(y, dest, tg, x2, gate, seq)


def _dsa_index_kernel(qi_ref, k2_ref, wi_ref, mask_ref, ka_scr, kb_scr, key_scr,
                      *, S, KC, n_sel, idx_scale):
    qb = pl.program_id(1)
    R = qi_ref.shape[1]
    nchunks = S // KC
    lane = lax.broadcasted_iota(I32, (1, LANES), 1)

    @pl.when(qb == 0)
    def _():
        kk = k2_ref[0]
        ka_scr[...] = jnp.where(lane < HEAD_DIM, kk, 0.0).astype(BF16)
        kb_scr[...] = jnp.where(lane >= HEAD_DIM, kk, 0.0).astype(BF16)

    nck = (qb * R + R + KC - 1) // KC
    w = wi_ref[0].astype(BF16).astype(F32)
    qv = qi_ref[0].astype(BF16)

    def relu_bf16(d):
        return jnp.maximum(d, 0.0).astype(BF16).astype(F32)

    row = qb * R + lax.broadcasted_iota(I32, (R, KC), 0)
    col0 = lax.broadcasted_iota(I32, (R, KC), 1)

    def sort_key(x):
        bits = pltpu.bitcast(x, I32)
        return jnp.where(bits < 0, bits ^ jnp.int32(0x7FFFFFFF), bits)

    def score_chunk(c, gmax):
        off = pl.multiple_of(c * KC, KC)
        ka = ka_scr[pl.ds(off, KC), :]
        kb = kb_scr[pl.ds(off, KC), :]
        sc = jnp.zeros((R, KC), F32)
        for pp in range(N_IDX_HEADS // 2):
            qp = qv[:, pp * LANES:(pp + 1) * LANES]
            sc = sc + w[:, 2 * pp:2 * pp + 1] * relu_bf16(_dot_nt(qp, ka))
            sc = sc + w[:, 2 * pp + 1:2 * pp + 2] * relu_bf16(_dot_nt(qp, kb))
        sc = sc * idx_scale
        sc = jnp.where(sc == 0.0, 0.0, sc)
        key_scr[c] = jnp.where(off + col0 <= row, sort_key(sc), INT_MIN)
        return gmax

    lax.fori_loop(0, nck, score_chunk, 0)

    def count(pred):
        def body(c, acc):
            kc = key_scr[c]
            for j in range(KC // LANES):
                acc = acc + jnp.where(pred(kc[:, j * LANES:(j + 1) * LANES], c * KC + j * LANES), 1.0, 0.0)
            return acc
        acc = lax.fori_loop(0, nck, body, jnp.zeros((R, LANES), F32))
        return jnp.broadcast_to(jnp.sum(acc, axis=1, keepdims=True), (R, LANES))

    nsel_f = float(n_sel)

    def bit_step(t, carry):
        cur, cnt_cur = carry
        cand = cur + (jnp.int32(1) << (31 - t))
        cnt = count(lambda kc, base: kc >= cand)
        ok = cnt >= nsel_f
        return jnp.where(ok, cand, cur), jnp.where(ok, cnt, cnt_cur)

    tau, cnt_tau = lax.fori_loop(
        0, 32, bit_step,
        (jnp.full((R, LANES), INT_MIN, I32), jnp.full((R, LANES), float(S), F32)))

    overflow = jnp.logical_and(tau > INT_MIN, cnt_tau > nsel_f)
    any_over = jnp.max(jnp.where(overflow, 1.0, 0.0)) > 0.0
    lane_r = lax.broadcasted_iota(I32, (R, LANES), 1)

    def write_mask(keep_fn):
        for c in range(nchunks):
            @pl.when(c < nck)
            def _():
                kc = key_scr[c]
                for j in range(KC // LANES):
                    keep = keep_fn(kc[:, j * LANES:(j + 1) * LANES], c * KC + j * LANES + lane_r)
                    mask_ref[0, :, c * KC + j * LANES:c * KC + (j + 1) * LANES] = (
                        jnp.where(keep, 1, 0).astype(jnp.int8))

            @pl.when(c >= nck)
            def _():
                mask_ref[0, :, c * KC:(c + 1) * KC] = jnp.zeros((R, KC), jnp.int8)

    @pl.when(jnp.logical_not(any_over))
    def _():
        floor = jnp.maximum(tau, INT_MIN + 1)
        write_mask(lambda kj, pos: kj >= floor)

    @pl.when(any_over)
    def _():
        need = nsel_f - count(lambda kc, base: kc > tau)

        def jstep(t, cur):
            cand = cur + (jnp.int32(1) << (S.bit_length() - 2 - t))
            cnt = count(lambda kc, base: jnp.logical_and(kc == tau, base + lane_r < cand))
            return jnp.where(cnt < need, cand, cur)

        jlim = lax.fori_loop(0, S.bit_length() - 1, jstep, jnp.zeros((R, LANES), I32))
        write_mask(lambda kj, pos: jnp.logical_and(
            jnp.logical_or(kj > tau, jnp.logical_and(kj == tau, pos <= jlim)), kj > INT_MIN))


def dsa_index_mask(aux, qcol, kcol, wcol, n_sel, R=128, KC=512):
    bsz, seq, _ = aux.shape
    R = min(R, seq)
    KC = min(KC, seq)
    nq = N_IDX_HEADS * HEAD_DIM
    return pl.pallas_call(
        functools.partial(_dsa_index_kernel, S=seq, KC=KC, n_sel=n_sel,
                          idx_scale=float(nq ** -0.5)),
        out_shape=jax.ShapeDtypeStruct((bsz, seq, seq), jnp.int8),
        grid=(bsz, seq // R),
        in_specs=[pl.BlockSpec((1, R, nq), lambda b, q: (b, q, qcol // nq)),
                  pl.BlockSpec((1, seq, LANES), lambda b, q: (b, 0, kcol // LANES)),
                  pl.BlockSpec((1, R, LANES), lambda b, q: (b, q, wcol // LANES))],
        out_specs=pl.BlockSpec((1, R, seq), lambda b, q: (b, q, 0)),
        scratch_shapes=[pltpu.VMEM((seq, LANES), BF16), pltpu.VMEM((seq, LANES), BF16),
                        pltpu.VMEM((seq // KC, R, KC), I32)],
        compiler_params=_cparams(("arbitrary", "arbitrary"), VMEM_LIMIT_BYTES),
        name="dsa_index_mask",
    )(aux, aux, aux)


def _dsa_attn_kernel(qi_ref, ki_ref, q_ref, k_ref, v_ref, mask_ref, tz_ref, o_ref,
                     m_scr, l_scr, acc_scr, *, TQ, TK, DCL):
    s = pl.program_id(1)
    qi = qi_ref[s]
    ki = ki_ref[s]
    k_last = ((qi + 1) * TQ - 1) // TK

    @pl.when(ki == 0)
    def _():
        m_scr[...] = jnp.full_like(m_scr, NEG)
        l_scr[...] = jnp.zeros_like(l_scr)
        acc_scr[...] = jnp.zeros_like(acc_scr)

    lane = lax.broadcasted_iota(I32, (1, LANES), 1)
    lo_half = lane < HEAD_DIM
    d0 = qi * TQ - ki * TK
    cb0 = (DCL - jnp.minimum(d0, DCL)) // LANES
    madd = jnp.where(mask_ref[0].astype(I32) != 0, 0.0, NEG).astype(BF16)
    npair = q_ref.shape[2] // LANES

    def logits(h):
        cs = slice((h // 2) * LANES, (h // 2 + 1) * LANES)
        qsel = lo_half if h % 2 == 0 else jnp.logical_not(lo_half)
        q = _keep_lanes(q_ref[0, :, cs] * QK_SCALE, qsel)
        bias = jnp.concatenate([tz_ref[h, cb0 + jj] for jj in range(TK // LANES)], axis=1)
        return _dot_nt(q, k_ref[0, :, cs]).astype(BF16) + bias + madd

    def weighted_values(h, pm):
        return _dot(pm, v_ref[0, :, h * LANES:(h + 1) * LANES])

    def accumulate(pp, alphas, pvs):
        cs = slice(pp * LANES, (pp + 1) * LANES)
        acc_scr[:, cs] = (acc_scr[:, cs] * jnp.where(lo_half, alphas[0], alphas[1])
                          + jnp.where(lo_half, pvs[0], pvs[1]))
        l_scr[pp] = (l_scr[pp] * jnp.where(lo_half, alphas[1], alphas[0])
                     + jnp.where(lo_half, pvs[1], pvs[0]))

    ahead = 2
    nh = 2 * npair
    queue = [logits(h) for h in range(ahead)]
    alphas, pvs, probs = {}, {}, {}
    for h in range(nh + 1):
        if h < nh:
            sc = queue.pop(0)
            if h + ahead < nh:
                queue.append(logits(h + ahead))
            m_scr[h], alphas[h], probs[h] = _online_softmax(sc, m_scr[h])
        if h >= 1:
            pvs[h - 1] = weighted_values(h - 1, probs.pop(h - 1))
            if (h - 1) % 2 == 1:
                accumulate((h - 1) // 2, (alphas[h - 2], alphas[h - 1]), (pvs[h - 2], pvs[h - 1]))

    @pl.when(ki == k_last)
    def _():
        for pp in range(npair):
            cs = slice(pp * LANES, (pp + 1) * LANES)
            l = pltpu.roll(l_scr[pp], HEAD_DIM, axis=1)
            o_ref[0, :, cs] = (acc_scr[:, cs] / l).astype(o_ref.dtype)


def _dsa_bias_table(rel_bias, seq, TQ, TK, DCL):
    i = np.arange(TQ)[:, None]
    c = np.arange(DCL + TK)[None, :]
    d = np.clip(i - c + DCL, 0, seq - 1)
    tz = _bucket_bias(jnp.asarray(d, I32), rel_bias)
    tz = tz.reshape(rel_bias.shape[1], TQ, (DCL + TK) // LANES, LANES)
    return tz.transpose(0, 2, 1, 3).astype(BF16)


def _np_t5_bucket(d):
    max_exact = N_BUCKETS // 2
    df = np.maximum(d, 1).astype(np.float32)
    large = max_exact + (np.log(df / max_exact) / math.log(MAX_DISTANCE / max_exact)
                         * (N_BUCKETS - max_exact)).astype(np.int32)
    return np.where(d < max_exact, d, np.minimum(large, N_BUCKETS - 1))


def dsa_attention(q_src, qcol, kv, mask, rel_bias, TQ=256, TK=1024):
    bsz, seq, _ = q_src.shape
    TQ = min(TQ, seq)
    TK = min(TK, seq)
    hw = N_DSA_HEADS * HEAD_DIM
    buckets = _np_t5_bucket(np.arange(seq))
    first_const = int(np.max(np.nonzero(buckets != buckets[-1])[0])) + 1
    DCL = min(seq, -(-(first_const + TK - 1) // LANES) * LANES)
    tz = _dsa_bias_table(rel_bias, seq, TQ, TK, DCL)
    qs, ks = _tri_schedule(seq // TQ, TK // TQ)
    grid_spec = pltpu.PrefetchScalarGridSpec(
        num_scalar_prefetch=2,
        grid=(bsz, qs.shape[0]),
        in_specs=[
            pl.BlockSpec((1, TQ, hw), lambda b, s, qi, ki: (b, qi[s], qcol // hw)),
            pl.BlockSpec((1, TK, hw), lambda b, s, qi, ki: (b, ki[s], 0)),
            pl.BlockSpec((1, TK, 2 * hw), lambda b, s, qi, ki: (b, ki[s], 0)),
            pl.BlockSpec((1, TQ, TK), lambda b, s, qi, ki: (b, qi[s], ki[s])),
            pl.BlockSpec(tz.shape, lambda b, s, qi, ki: (0, 0, 0, 0), pipeline_mode=pl.Buffered(1)),
        ],
        out_specs=pl.BlockSpec((1, TQ, hw), lambda b, s, qi, ki: (b, qi[s], 0)),
        scratch_shapes=[pltpu.VMEM((N_DSA_HEADS, TQ, LANES), F32),
                        pltpu.VMEM((N_DSA_HEADS // 2, TQ, LANES), F32),
                        pltpu.VMEM((TQ, hw), F32)],
    )
    return pl.pallas_call(
        functools.partial(_dsa_attn_kernel, TQ=TQ, TK=TK, DCL=DCL),
        out_shape=jax.ShapeDtypeStruct((bsz, seq, hw), BF16),
        grid_spec=grid_spec,
        compiler_params=_cparams(("arbitrary", "arbitrary"), VMEM_LIMIT_BYTES),
        name="dsa_attention",
    )(qs, ks, q_src, kv, _values_with_ones(kv[:, :, hw:], N_DSA_HEADS), mask, tz)


def _rms_kernel(x_ref, g_ref, o_ref):
    x = x_ref[...]
    ms = jnp.mean(x * x, axis=-1, keepdims=True)
    o_ref[...] = x * lax.rsqrt(ms + EPS) * g_ref[...]


def final_rmsnorm(x2, g, tm=512):
    n_tok, d = x2.shape
    tm = min(tm, n_tok)
    return pl.pallas_call(
        _rms_kernel,
        out_shape=jax.ShapeDtypeStruct((n_tok, d), F32),
        grid=(n_tok // tm,),
        in_specs=[pl.BlockSpec((tm, d), lambda i: (i, 0)), pl.BlockSpec((1, d), lambda i: (0, 0))],
        out_specs=pl.BlockSpec((tm, d), lambda i: (i, 0)),
        compiler_params=_cparams(("arbitrary",)),
        name="final_rmsnorm",
    )(x2, g.reshape(1, d))


def even_mixer_residual(x2, shift1, scale1, gate1, norm1, w_in, fox_fb, w_out, rel_bias, bsz, seq):
    d = x2.shape[1]
    nfq = 3 * N_FOX_HEADS * HEAD_DIM
    w_main = jnp.concatenate([w_in[:, :nfq], w_in[:, nfq + N_FOX_HEADS:]], axis=1).astype(BF16)
    rep = np.repeat(np.arange(N_FOX_HEADS), FOX_PARTS)
    gcols = np.concatenate([rep, rep])
    w_gate = jnp.zeros((d, LANES), F32).at[:, :2 * FOX_XW].set(w_in[:, nfq + gcols])
    fb = jnp.zeros((1, LANES), F32).at[0, :2 * FOX_XW].set(fox_fb[gcols])
    proj, gate_z = norm_mod_matmul(x2, norm1, scale1, shift1, w_main, w_gate, seq)
    c = proj.shape[1]
    proj3 = proj.reshape(bsz, seq, c)
    qx, kx = fox_gate_columns(gate_z.reshape(bsz, seq, LANES), fb)
    fox = fox_attention(proj3, qx, kx, N_FOX_HEADS)
    dil_o, dil_l = [], []
    for g, (window, dil) in enumerate(DIL_PAIRS):
        bias = _dil_bias_tiles(rel_bias, g, window, dil)
        o, l = dilated_group_attention(proj3, bias, g, dil, nfq + g * N_DIL_SLOTS * HEAD_DIM)
        dil_o.append(o)
        dil_l.append(l)
    return out_proj_even(fox.reshape(bsz * seq, -1), dil_o, dil_l, w_out.astype(BF16), x2, gate1, seq)


def odd_mixer_residual(x2, shift1, scale1, gate1, norm1, w_in, kv_norm, w_ukv, w_out, rel_bias, bsz, seq):
    d = x2.shape[1]
    nq = N_DSA_HEADS * HEAD_DIM
    ni = N_IDX_HEADS * HEAD_DIM
    c_ckv, c_qi = nq, nq + KV_RANK
    c_ki = c_qi + ni
    c_wi = c_ki + HEAD_DIM
    w_main = w_in[:, :nq].astype(BF16)
    c_w, c_k, c_q = KV_RANK, KV_RANK + LANES, KV_RANK + 2 * LANES
    w_aux = jnp.zeros((d, c_q + ni), F32)
    w_aux = w_aux.at[:, :KV_RANK].set(w_in[:, c_ckv:c_qi])
    w_aux = w_aux.at[:, c_w:c_w + N_IDX_HEADS].set(w_in[:, c_wi:])
    w_aux = w_aux.at[:, c_k:c_k + HEAD_DIM].set(w_in[:, c_ki:c_wi])
    w_aux = w_aux.at[:, c_k + HEAD_DIM:c_q].set(w_in[:, c_ki:c_wi])
    w_aux = w_aux.at[:, c_q:].set(w_in[:, c_qi:c_ki])
    proj, aux = norm_mod_matmul(x2, norm1, scale1, shift1, w_main, w_aux, seq)
    zeros = jnp.zeros((bsz, KV_RANK), F32)
    kv = norm_mod_matmul(aux[:, :KV_RANK], kv_norm, zeros, zeros, w_ukv.astype(BF16), None, seq)
    proj3 = proj.reshape(bsz, seq, -1)
    n_sel = min(DSA_TOPK, seq // 4)
    mask = dsa_index_mask(aux.reshape(bsz, seq, -1), c_q, c_k, c_w, n_sel)
    att = dsa_attention(proj3, 0, kv.reshape(bsz, seq, -1), mask, rel_bias)
    return out_proj_odd(att.reshape(bsz * seq, -1), w_out.astype(BF16), x2, gate1, seq)


def kernel(x, c, rel_bias, l0_norm1, l0_ada_w, l0_ada_b, l0_w_in, l0_fox_fb, l0_w_out, l0_norm2, l0_router_w, l0_router_b, l0_w1, l0_b1, l0_w2, l0_b2, l1_norm1, l1_ada_w, l1_ada_b, l1_w_in, l1_kv_norm, l1_w_ukv, l1_w_out, l1_norm2, l1_router_w, l1_router_b, l1_w1, l1_b1, l1_w2, l1_b2, final_norm):
    bsz, seq, d = x.shape
    x2 = x.reshape(bsz * seq, d)
    layers = (
        (l0_norm1, l0_ada_w, l0_ada_b, l0_norm2, l0_router_w, l0_router_b, l0_w1, l0_b1, l0_w2, l0_b2),
        (l1_norm1, l1_ada_w, l1_ada_b, l1_norm2, l1_router_w, l1_router_b, l1_w1, l1_b1, l1_w2, l1_b2),
    )
    for i, (norm1, ada_w, ada_b, norm2, rw, rb, w1, b1, w2, b2) in enumerate(layers):
        mods = ada_mods(c, ada_w, ada_b)
        shift1, scale1, gate1, shift2, scale2, gate2 = (mods[:, j * d:(j + 1) * d] for j in range(6))
        if i % 2 == 0:
            x2 = even_mixer_residual(x2, shift1, scale1, gate1, norm1, l0_w_in, l0_fox_fb, l0_w_out,
                                     rel_bias, bsz, seq)
        else:
            x2 = odd_mixer_residual(x2, shift1, scale1, gate1, norm1, l1_w_in, l1_kv_norm, l1_w_ukv,
                                    l1_w_out, rel_bias, bsz, seq)
        x2 = moe_block(x2, norm2, scale2, shift2, gate2, rw, rb, w1, b1, w2, b2, seq)
    return final_rmsnorm(x2, final_norm).reshape(bsz, seq, d)
```
